```python
import math
import jax, jax.numpy as jnp
from jax import lax
import numpy as np

D_MODEL = 2048
BATCH = 4
SEQ = 2048
DEPTH = 2
DEC_BATCH = 128
DEC_SEQ = 8
PAST_LEN = 16384
PAGE_SIZE = 128

D_FF = 5632
D_PLE = 256
D_A = D_MODEL // 2
CONV_A_WIDTH = 31
D_B = D_MODEL // 2
S5_GROUP = 16
S5_GROUPS = D_B // S5_GROUP
S5_STATE = 64
D_C = D_MODEL
CONV_C_WIDTH = 3
N_EVEN = (DEPTH + 1) // 2
N_ODD = DEPTH // 2
EPS = 1e-6

kernel_name = 'hybrid_conformer_s5_shortconv_step'


def rms_norm(x, g):
    xf = x.astype(jnp.float32)
    y = xf * lax.rsqrt(jnp.mean(xf * xf, axis=-1, keepdims=True) + EPS)
    return (y * g.astype(jnp.float32)).astype(x.dtype)


def layer_norm(x, g, b):
    xf = x.astype(jnp.float32)
    mu = jnp.mean(xf, axis=-1, keepdims=True)
    xc = xf - mu
    y = xc * lax.rsqrt(jnp.mean(xc * xc, axis=-1, keepdims=True) + EPS)
    return (y * g.astype(jnp.float32) + b.astype(jnp.float32)).astype(x.dtype)


def swiglu(x, wg, wu, wd):
    return (jax.nn.silu(x @ wg) * (x @ wu)) @ wd


def causal_dwconv(x, prev, w):
    k = w.shape[0]
    xp = jnp.concatenate([prev.astype(x.dtype), x], axis=1)
    y = lax.conv_general_dilated(xp, w[:, None, :].astype(x.dtype), window_strides=(1,), padding='VALID',
                                 dimension_numbers=('NWC', 'WIO', 'NWC'), feature_group_count=x.shape[-1])
    return y, xp[:, xp.shape[1] - (k - 1):]


def _cmul(ar, ai, br, bi):
    return ar * br - ai * bi, ar * bi + ai * br


def s5_scan(u, h0_re, h0_im, lam_re, lam_im, log_dt, b_re, b_im, c_re, c_im, d):
    f32 = jnp.float32
    lam_re, lam_im = lam_re.astype(f32), lam_im.astype(f32)
    dt = jnp.exp(log_dt.astype(f32))[:, None]
    mag = jnp.exp(lam_re * dt)
    ar, ai = mag * jnp.cos(lam_im * dt), mag * jnp.sin(lam_im * dt)
    den = lam_re * lam_re + lam_im * lam_im
    qr = ((ar - 1.0) * lam_re + ai * lam_im) / den
    qi = (ai * lam_re - (ar - 1.0) * lam_im) / den
    bbr, bbi = _cmul(qr[..., None], qi[..., None], b_re.astype(f32), b_im.astype(f32))
    bu_re = jnp.einsum('btgh,gnh->btgn', u, bbr)
    bu_im = jnp.einsum('btgh,gnh->btgn', u, bbi)
    i0r, i0i = _cmul(ar, ai, h0_re.astype(f32), h0_im.astype(f32))
    bu_re = bu_re.at[:, 0].add(i0r)
    bu_im = bu_im.at[:, 0].add(i0i)
    a_re = jnp.broadcast_to(ar, bu_re.shape)
    a_im = jnp.broadcast_to(ai, bu_im.shape)

    def combine(e1, e2):
        a1r, a1i, b1r, b1i = e1
        a2r, a2i, b2r, b2i = e2
        nar, nai = _cmul(a2r, a2i, a1r, a1i)
        tbr, tbi = _cmul(a2r, a2i, b1r, b1i)
        return nar, nai, tbr + b2r, tbi + b2i

    _, _, h_re, h_im = lax.associative_scan(combine, (a_re, a_im, bu_re, bu_im), axis=1)
    y = (jnp.einsum('btgn,ghn->btgh', h_re, c_re.astype(f32))
         - jnp.einsum('btgn,ghn->btgh', h_im, c_im.astype(f32))
         + d.astype(f32) * u)
    return y, h_re[:, -1], h_im[:, -1]


def mixer_ab(h, conv_prev, s_re, s_im, w_in, w_dw, b_dw, ln_g, ln_b, lam_re, lam_im, log_dt,
             b_re, b_im, c_re, c_im, d, w_glu, b_glu, w_out):
    bsz, t = h.shape[0], h.shape[1]
    z = h @ w_in
    a_val, a_gate, u = z[..., :D_A], z[..., D_A:2 * D_A], z[..., 2 * D_A:]
    a = a_val * jax.nn.sigmoid(a_gate)
    a, conv_new = causal_dwconv(a, conv_prev, w_dw)
    a = jax.nn.silu(layer_norm(a + b_dw, ln_g, ln_b))
    ub = u.astype(jnp.float32).reshape(bsz, t, S5_GROUPS, S5_GROUP)
    yb, hr, hi = s5_scan(ub, s_re, s_im, lam_re, lam_im, log_dt, b_re, b_im, c_re, c_im,
                         d.reshape(S5_GROUPS, S5_GROUP))
    yb = jax.nn.gelu(yb.reshape(bsz, t, D_B).astype(h.dtype))
    yb = yb * jax.nn.sigmoid(yb @ w_glu + b_glu)
    out = jnp.concatenate([a, yb], axis=-1) @ w_out
    return out, conv_new, hr, hi


def mixer_c(h, conv_prev, w_in, w_dw, w_out):
    z = h @ w_in
    bg, cg, v = z[..., :D_C], z[..., D_C:2 * D_C], z[..., 2 * D_C:]
    xc, conv_new = causal_dwconv(cg * v, conv_prev, w_dw)
    return (bg * xc) @ w_out, conv_new


def trunk(x, p, conv_a, s5_re, s5_im, conv_c, w):
    new_ca, new_sr, new_si, new_cc = [], [], [], []
    for i in range(DEPTH):
        j = i // 2
        h = x + 0.5 * swiglu(rms_norm(x, w['norm_ffn1'][i]), w['w_ffn1_gate'][i], w['w_ffn1_up'][i], w['w_ffn1_down'][i])
        hn = rms_norm(h, w['norm_mix'][i])
        if i % 2 == 0:
            m, ca, sr, si = mixer_ab(hn, conv_a[j], s5_re[j], s5_im[j], w['w_in_ab'][j], w['w_dw_a'][j], w['b_dw_a'][j],
                                     w['ln_a_g'][j], w['ln_a_b'][j], w['s5_lam_re'][j], w['s5_lam_im'][j],
                                     w['s5_log_dt'][j], w['s5_b_re'][j], w['s5_b_im'][j], w['s5_c_re'][j],
                                     w['s5_c_im'][j], w['s5_d'][j], w['w_glu_b'][j], w['b_glu_b'][j], w['w_out_ab'][j])
            new_ca.append(ca)
            new_sr.append(sr)
            new_si.append(si)
        else:
            m, cc = mixer_c(hn, conv_c[j], w['w_in_c'][j], w['w_dw_c'][j], w['w_out_c'][j])
            new_cc.append(cc)
        h = h + m
        h = h + 0.5 * swiglu(rms_norm(h, w['norm_ffn2'][i]), w['w_ffn2_gate'][i], w['w_ffn2_up'][i], w['w_ffn2_down'][i])
        gate = jax.nn.sigmoid(rms_norm(h, w['norm_ple'][i]) @ w['w_ple_gate'][i])
        x = h + gate * (p[i] @ w['w_ple_proj'][i])
    y = rms_norm(x, w['norm_final'])
    return y, jnp.stack(new_ca), jnp.stack(new_sr), jnp.stack(new_si), jnp.stack(new_cc)


def setup_inputs(seed: int = 0) -> dict:
    key = jax.random.key(seed)
    ks = iter(jax.random.split(key, 64))
    f32 = jnp.float32

    def nrm(shape, scale):
        return scale * jax.random.normal(next(ks), shape, f32)

    def gain(shape):
        return 1.0 + nrm(shape, 0.05)

    G, N, H = S5_GROUPS, S5_STATE, S5_GROUP
    n_idx = jnp.arange(N, dtype=f32)
    inp = {}
    inp['x_prompt'] = nrm((BATCH, SEQ, D_MODEL), 1.0)
    inp['x_sample'] = nrm((DEC_BATCH, DEC_SEQ, D_MODEL), 1.0)
    inp['p_prompt'] = nrm((DEPTH, BATCH, SEQ, D_PLE), 1.0)
    inp['p_sample'] = nrm((DEPTH, DEC_BATCH, DEC_SEQ, D_PLE), 1.0)
    inp['state_convA'] = nrm((N_EVEN, DEC_BATCH, CONV_A_WIDTH - 1, D_A), 0.5)
    inp['state_s5_re'] = nrm((N_EVEN, DEC_BATCH, G, N), 0.3)
    inp['state_s5_im'] = nrm((N_EVEN, DEC_BATCH, G, N), 0.3)
    inp['state_convC'] = nrm((N_ODD, DEC_BATCH, CONV_C_WIDTH - 1, D_C), 1.0)
    inp['norm_ffn1'] = gain((DEPTH, D_MODEL))
    inp['w_ffn1_gate'] = nrm((DEPTH, D_MODEL, D_FF), D_MODEL ** -0.5)
    inp['w_ffn1_up'] = nrm((DEPTH, D_MODEL, D_FF), D_MODEL ** -0.5)
    inp['w_ffn1_down'] = nrm((DEPTH, D_FF, D_MODEL), D_FF ** -0.5)
    inp['norm_mix'] = gain((DEPTH, D_MODEL))
    inp['norm_ffn2'] = gain((DEPTH, D_MODEL))
    inp['w_ffn2_gate'] = nrm((DEPTH, D_MODEL, D_FF), D_MODEL ** -0.5)
    inp['w_ffn2_up'] = nrm((DEPTH, D_MODEL, D_FF), D_MODEL ** -0.5)
    inp['w_ffn2_down'] = nrm((DEPTH, D_FF, D_MODEL), D_FF ** -0.5)
    inp['norm_ple'] = gain((DEPTH, D_MODEL))
    inp['w_ple_gate'] = nrm((DEPTH, D_MODEL, D_MODEL), D_MODEL ** -0.5)
    inp['w_ple_proj'] = nrm((DEPTH, D_PLE, D_MODEL), D_PLE ** -0.5)
    inp['w_in_ab'] = nrm((N_EVEN, D_MODEL, 2 * D_A + D_B), D_MODEL ** -0.5)
    inp['w_dw_a'] = nrm((N_EVEN, CONV_A_WIDTH, D_A), CONV_A_WIDTH ** -0.5)
    inp['b_dw_a'] = nrm((N_EVEN, D_A), 0.02)
    inp['ln_a_g'] = gain((N_EVEN, D_A))
    inp['ln_a_b'] = nrm((N_EVEN, D_A), 0.02)
    inp['s5_lam_re'] = -0.5 + nrm((N_EVEN, G, N), 0.005)
    inp['s5_lam_im'] = math.pi * n_idx + nrm((N_EVEN, G, N), 0.01)
    inp['s5_log_dt'] = jax.random.uniform(next(ks), (N_EVEN, G), f32, minval=math.log(0.01), maxval=math.log(0.1))
    inp['s5_b_re'] = nrm((N_EVEN, G, N, H), (2 * H) ** -0.5)
    inp['s5_b_im'] = nrm((N_EVEN, G, N, H), (2 * H) ** -0.5)
    inp['s5_c_re'] = nrm((N_EVEN, G, H, N), N ** -0.5)
    inp['s5_c_im'] = nrm((N_EVEN, G, H, N), N ** -0.5)
    inp['s5_d'] = nrm((N_EVEN, D_B), 1.0)
    inp['w_glu_b'] = nrm((N_EVEN, D_B, D_B), D_B ** -0.5)
    inp['b_glu_b'] = nrm((N_EVEN, D_B), 0.02)
    inp['w_out_ab'] = nrm((N_EVEN, D_A + D_B, D_MODEL), (D_A + D_B) ** -0.5)
    inp['w_in_c'] = nrm((N_ODD, D_MODEL, 3 * D_C), D_MODEL ** -0.5)
    inp['w_dw_c'] = nrm((N_ODD, CONV_C_WIDTH, D_C), CONV_C_WIDTH ** -0.5)
    inp['w_out_c'] = nrm((N_ODD, D_C, D_MODEL), D_C ** -0.5)
    inp['norm_final'] = gain((D_MODEL,))
    return inp


def reference(x_prompt, x_sample, p_prompt, p_sample, state_convA, state_s5_re, state_s5_im, state_convC,
              norm_ffn1, w_ffn1_gate, w_ffn1_up, w_ffn1_down, norm_mix, norm_ffn2, w_ffn2_gate, w_ffn2_up,
              w_ffn2_down, norm_ple, w_ple_gate, w_ple_proj, w_in_ab, w_dw_a, b_dw_a, ln_a_g, ln_a_b,
              s5_lam_re, s5_lam_im, s5_log_dt, s5_b_re, s5_b_im, s5_c_re, s5_c_im, s5_d, w_glu_b, b_glu_b,
              w_out_ab, w_in_c, w_dw_c, w_out_c, norm_final):
    w = dict(norm_ffn1=norm_ffn1, w_ffn1_gate=w_ffn1_gate, w_ffn1_up=w_ffn1_up, w_ffn1_down=w_ffn1_down,
             norm_mix=norm_mix, norm_ffn2=norm_ffn2, w_ffn2_gate=w_ffn2_gate, w_ffn2_up=w_ffn2_up,
             w_ffn2_down=w_ffn2_down, norm_ple=norm_ple, w_ple_gate=w_ple_gate, w_ple_proj=w_ple_proj,
             w_in_ab=w_in_ab, w_dw_a=w_dw_a, b_dw_a=b_dw_a, ln_a_g=ln_a_g, ln_a_b=ln_a_b,
             s5_lam_re=s5_lam_re, s5_lam_im=s5_lam_im, s5_log_dt=s5_log_dt, s5_b_re=s5_b_re, s5_b_im=s5_b_im,
             s5_c_re=s5_c_re, s5_c_im=s5_c_im, s5_d=s5_d, w_glu_b=w_glu_b, b_glu_b=b_glu_b, w_out_ab=w_out_ab,
             w_in_c=w_in_c, w_dw_c=w_dw_c, w_out_c=w_out_c, norm_final=norm_final)
    bp = x_prompt.shape[0]
    ca0 = jnp.zeros((N_EVEN, bp, CONV_A_WIDTH - 1, D_A), x_prompt.dtype)
    sr0 = jnp.zeros((N_EVEN, bp, S5_GROUPS, S5_STATE), jnp.float32)
    si0 = jnp.zeros((N_EVEN, bp, S5_GROUPS, S5_STATE), jnp.float32)
    cc0 = jnp.zeros((N_ODD, bp, CONV_C_WIDTH - 1, D_C), x_prompt.dtype)
    y_prompt, ca_p, sr_p, si_p, cc_p = trunk(x_prompt, p_prompt, ca0, sr0, si0, cc0, w)
    y_sample, ca_s, sr_s, si_s, cc_s = trunk(x_sample, p_sample, state_convA, state_s5_re, state_s5_im, state_convC, w)
    return (y_prompt, y_sample, ca_p, ca_s, sr_p, si_p, sr_s, si_s, cc_p, cc_s)
```

```python
import functools

import jax
import jax.numpy as jnp
from jax import lax
from jax.experimental import pallas as pl
from jax.experimental.pallas import tpu as pltpu

F32 = jnp.float32
BF16 = jnp.bfloat16
EPS = 1e-6

LANES = 128
SUBLANES = 8
VMEM_LIMIT = 56 * 1024 * 1024

S5_GROUP = 16
S5_STATE = 64
S5_TILE_GROUPS = 16
CONV_A_WIDTH = 31
CONV_C_WIDTH = 3
CONV_A_PAD = 32
CONV_C_PAD = 8


def _dot(a, b):
    return jnp.dot(a, b, preferred_element_type=F32)


def _rmsnorm_bf16(x, g):
    ms = jnp.mean(x * x, axis=-1, keepdims=True)
    return (x * lax.rsqrt(ms + EPS) * g).astype(BF16)


def _params(*sem):
    return pltpu.CompilerParams(dimension_semantics=sem, vmem_limit_bytes=VMEM_LIMIT)


def _resident(shape):
    nd = len(shape)
    return pl.BlockSpec(shape, lambda *_: (0,) * nd, pipeline_mode=pl.Buffered(1))


def _ffn_kernel(x_ref, g_ref, wg_ref, wu_ref, wd_ref, o_ref, xn_ref, acc_ref, *, nf):
    j = pl.program_id(1)

    @pl.when(j == 0)
    def _():
        xn_ref[...] = _rmsnorm_bf16(x_ref[...], g_ref[...])
        acc_ref[...] = jnp.zeros_like(acc_ref)

    xn = xn_ref[...]
    gate = _dot(xn, wg_ref[...])
    up = _dot(xn, wu_ref[...])
    act = (gate * jax.nn.sigmoid(gate) * up).astype(BF16)
    acc_ref[...] += _dot(act, wd_ref[...])

    @pl.when(j == nf - 1)
    def _():
        o_ref[...] = x_ref[...] + 0.5 * acc_ref[...]


def _ffn(x, g, wg, wu, wd, *, tm=512, tf=512):
    m, d = x.shape
    f = wg.shape[1]
    nf = f // tf
    return pl.pallas_call(
        functools.partial(_ffn_kernel, nf=nf),
        grid=(m // tm, nf),
        in_specs=[
            pl.BlockSpec((tm, d), lambda i, j: (i, 0)),
            pl.BlockSpec((1, d), lambda i, j: (0, 0)),
            pl.BlockSpec((d, tf), lambda i, j: (0, j)),
            pl.BlockSpec((d, tf), lambda i, j: (0, j)),
            pl.BlockSpec((tf, d), lambda i, j: (j, 0)),
        ],
        out_specs=pl.BlockSpec((tm, d), lambda i, j: (i, 0)),
        out_shape=jax.ShapeDtypeStruct((m, d), F32),
        scratch_shapes=[pltpu.VMEM((tm, d), BF16), pltpu.VMEM((tm, d), F32)],
        compiler_params=_params("parallel", "arbitrary"),
        name="ffn",
    )(x, g, wg, wu, wd)


def _inab_kernel(x_ref, g_ref, wv_ref, wgt_ref, wu_ref, a_ref, u_ref, xn_ref):
    @pl.when(pl.program_id(1) == 0)
    def _():
        xn_ref[...] = _rmsnorm_bf16(x_ref[...], g_ref[...])

    xn = xn_ref[...]
    a_ref[...] = _dot(xn, wv_ref[...]) * jax.nn.sigmoid(_dot(xn, wgt_ref[...]))
    u_ref[...] = _dot(xn, wu_ref[...])


def _in_ab(x, g, w, *, tm=512, tn=512):
    m, d = x.shape
    c = w.shape[1] // 3
    nb = c // tn
    wspec = lambda k: pl.BlockSpec((d, tn), lambda i, j: (0, j + k * nb))
    ospec = pl.BlockSpec((tm, tn), lambda i, j: (i, j))
    return pl.pallas_call(
        _inab_kernel,
        grid=(m // tm, nb),
        in_specs=[pl.BlockSpec((tm, d), lambda i, j: (i, 0)), pl.BlockSpec((1, d), lambda i, j: (0, 0)),
                  wspec(0), wspec(1), wspec(2)],
        out_specs=[ospec, ospec],
        out_shape=[jax.ShapeDtypeStruct((m, c), F32)] * 2,
        scratch_shapes=[pltpu.VMEM((tm, d), BF16)],
        compiler_params=_params("parallel", "arbitrary"),
        name="in_ab",
    )(x, g, w, w, w)


def _inc_kernel(x_ref, g_ref, wb_ref, wc_ref, wv_ref, bg_ref, cv_ref, xn_ref):
    @pl.when(pl.program_id(1) == 0)
    def _():
        xn_ref[...] = _rmsnorm_bf16(x_ref[...], g_ref[...])

    xn = xn_ref[...]
    bg_ref[...] = _dot(xn, wb_ref[...])
    cv_ref[...] = _dot(xn, wc_ref[...]) * _dot(xn, wv_ref[...])


def _in_c(x, g, w, *, tm=512, tn=512):
    m, d = x.shape
    c = w.shape[1] // 3
    nb = c // tn
    wspec = lambda k: pl.BlockSpec((d, tn), lambda i, j: (0, j + k * nb))
    ospec = pl.BlockSpec((tm, tn), lambda i, j: (i, j))
    return pl.pallas_call(
        _inc_kernel,
        grid=(m // tm, nb),
        in_specs=[pl.BlockSpec((tm, d), lambda i, j: (i, 0)), pl.BlockSpec((1, d), lambda i, j: (0, 0)),
                  wspec(0), wspec(1), wspec(2)],
        out_specs=[ospec, ospec],
        out_shape=[jax.ShapeDtypeStruct((m, c), F32)] * 2,
        scratch_shapes=[pltpu.VMEM((tm, d), BF16)],
        compiler_params=_params("parallel", "arbitrary"),
        name="in_c",
    )(x, g, w, w, w)


def _outproj_kernel(*refs, ksizes, tn):
    n = len(ksizes)
    act_refs, (w_ref, h_ref, o_ref) = refs[:n], refs[n:]
    acts = [a[...].astype(BF16) for a in act_refs]
    for c in range(o_ref.shape[1] // tn):
        cols = slice(c * tn, (c + 1) * tn)
        acc = h_ref[:, cols]
        off = 0
        for a, k in zip(acts, ksizes):
            acc = acc + _dot(a, w_ref[off:off + k, cols])
            off += k
        o_ref[:, cols] = acc


def _out_proj(acts, w, h, *, tm=512, tn=512):
    m, d = h.shape
    ksizes = tuple(a.shape[1] for a in acts)
    row = lambda k: pl.BlockSpec((tm, k), lambda i: (i, 0))
    return pl.pallas_call(
        functools.partial(_outproj_kernel, ksizes=ksizes, tn=tn),
        grid=(m // tm,),
        in_specs=[row(k) for k in ksizes] + [_resident(w.shape), row(d)],
        out_specs=row(d),
        out_shape=jax.ShapeDtypeStruct((m, d), F32),
        compiler_params=_params("parallel"),
        name="out_proj",
    )(*acts, w, h)


def _ple_kernel(*refs, final, tn):
    if final:
        x_ref, p_ref, g_ref, wg_ref, wp_ref, gf_ref, o_ref = refs
    else:
        x_ref, p_ref, g_ref, wg_ref, wp_ref, o_ref = refs
    d = o_ref.shape[1]
    xn = _rmsnorm_bf16(x_ref[...], g_ref[...])
    pb = p_ref[...].astype(BF16)
    ssq = jnp.zeros((o_ref.shape[0], 1), F32)
    for c in range(d // tn):
        cols = slice(c * tn, (c + 1) * tn)
        gate = jax.nn.sigmoid(_dot(xn, wg_ref[:, cols]))
        y = x_ref[:, cols] + gate * _dot(pb, wp_ref[:, cols])
        ssq = ssq + jnp.sum(y * y, axis=-1, keepdims=True)
        o_ref[:, cols] = y
    if final:
        o_ref[...] = o_ref[...] * lax.rsqrt(ssq / d + EPS) * gf_ref[...]


def _ple(x, p, layer, g, wg, wp, gf, *, tm=512, tn=512):
    m, d = x.shape
    dp = p.shape[-1]
    final = gf is not None
    in_specs = [
        pl.BlockSpec((tm, d), lambda i: (i, 0)),
        pl.BlockSpec((None, tm, dp), lambda i: (layer, i, 0)),
        _resident((1, d)),
        _resident(wg.shape),
        _resident(wp.shape),
    ]
    args = [x, p, g, wg, wp]
    if final:
        in_specs.append(_resident((1, d)))
        args.append(gf)
    return pl.pallas_call(
        functools.partial(_ple_kernel, final=final, tn=tn),
        grid=(m // tm,),
        in_specs=in_specs,
        out_specs=pl.BlockSpec((tm, d), lambda i: (i, 0)),
        out_shape=jax.ShapeDtypeStruct((m, d), F32),
        compiler_params=_params("parallel"),
        name="ple",
    )(*args)


def _conva_kernel(x_ref, prev_ref, w_ref, b_ref, g_ref, beta_ref, o_ref, st_ref, buf_ref, *, ns, tb, nt, rows):
    t = pl.program_id(1)
    pad, k_w = CONV_A_PAD, CONV_A_WIDTH
    first = pad - (k_w - 1)

    @pl.when(t == 0)
    def _():
        buf_ref[:, 0:pad, :] = prev_ref[...]

    buf_ref[:, pad:pad + tb, :] = x_ref[...]

    def per_seq(s, carry):
        for r in range(tb // rows):
            acc = None
            for k in range(k_w):
                term = buf_ref[s, pl.ds(r * rows + first + k, rows), :] * w_ref[k:k + 1, :]
                acc = term if acc is None else acc + term
            y = acc + b_ref[...]
            mu = jnp.mean(y, axis=-1, keepdims=True)
            yc = y - mu
            var = jnp.mean(yc * yc, axis=-1, keepdims=True)
            z = yc * lax.rsqrt(var + EPS) * g_ref[...] + beta_ref[...]
            o_ref[s, r * rows:(r + 1) * rows, :] = z * jax.nn.sigmoid(z)
        return carry

    lax.fori_loop(0, ns, per_seq, 0)

    @pl.when(t == nt - 1)
    def _():
        st_ref[...] = buf_ref[:, tb + first:tb + pad, :]

    if nt > 1:
        buf_ref[:, 0:pad, :] = buf_ref[:, tb:tb + pad, :]


def _conv_a(x, prev, w, b, g, beta, *, ns, tb, rows):
    nseq, t_len, c = x.shape
    nt = t_len // tb
    prev = jnp.pad(prev, ((0, 0), (CONV_A_PAD - (CONV_A_WIDTH - 1), 0), (0, 0)))
    return pl.pallas_call(
        functools.partial(_conva_kernel, ns=ns, tb=tb, nt=nt, rows=rows),
        grid=(nseq // ns, nt),
        in_specs=[
            pl.BlockSpec((ns, tb, c), lambda i, j: (i, j, 0)),
            pl.BlockSpec((ns, CONV_A_PAD, c), lambda i, j: (i, 0, 0)),
            pl.BlockSpec((CONV_A_WIDTH, c), lambda i, j: (0, 0)),
            pl.BlockSpec((1, c), lambda i, j: (0, 0)),
            pl.BlockSpec((1, c), lambda i, j: (0, 0)),
            pl.BlockSpec((1, c), lambda i, j: (0, 0)),
        ],
        out_specs=[
            pl.BlockSpec((ns, tb, c), lambda i, j: (i, j, 0)),
            pl.BlockSpec((ns, CONV_A_WIDTH - 1, c), lambda i, j: (i, 0, 0)),
        ],
        out_shape=[jax.ShapeDtypeStruct((nseq, t_len, c), F32),
                   jax.ShapeDtypeStruct((nseq, CONV_A_WIDTH - 1, c), F32)],
        scratch_shapes=[pltpu.VMEM((ns, CONV_A_PAD + tb, c), F32)],
        compiler_params=_params("parallel", "arbitrary"),
        name="conv_a",
    )(x, prev, w, b, g, beta)


def _convc_kernel(bg_ref, cv_ref, prev_ref, w_ref, o_ref, st_ref, buf_ref, *, ns, tb, nt, rows):
    t = pl.program_id(1)
    pad, k_w = CONV_C_PAD, CONV_C_WIDTH
    first = pad - (k_w - 1)

    @pl.when(t == 0)
    def _():
        buf_ref[:, 0:pad, :] = prev_ref[...]

    buf_ref[:, pad:pad + tb, :] = cv_ref[...]

    def per_seq(s, carry):
        for r in range(tb // rows):
            acc = None
            for k in range(k_w):
                term = buf_ref[s, pl.ds(r * rows + first + k, rows), :] * w_ref[k:k + 1, :]
                acc = term if acc is None else acc + term
            o_ref[s, r * rows:(r + 1) * rows, :] = bg_ref[s, r * rows:(r + 1) * rows, :] * acc
        return carry

    lax.fori_loop(0, ns, per_seq, 0)

    @pl.when(t == nt - 1)
    def _():
        st_ref[...] = buf_ref[:, tb + first:tb + pad, :]

    if nt > 1:
        buf_ref[:, 0:pad, :] = buf_ref[:, tb:tb + pad, :]


def _conv_c(bg, cv, prev, w, *, ns, tb, rows):
    nseq, t_len, c = cv.shape
    nt = t_len // tb
    prev = jnp.pad(prev, ((0, 0), (CONV_C_PAD - (CONV_C_WIDTH - 1), 0), (0, 0)))
    blk = pl.BlockSpec((ns, tb, c), lambda i, j: (i, j, 0))
    return pl.pallas_call(
        functools.partial(_convc_kernel, ns=ns, tb=tb, nt=nt, rows=rows),
        grid=(nseq // ns, nt),
        in_specs=[blk, blk,
                  pl.BlockSpec((ns, CONV_C_PAD, c), lambda i, j: (i, 0, 0)),
                  pl.BlockSpec((CONV_C_WIDTH, c), lambda i, j: (0, 0))],
        out_specs=[blk, pl.BlockSpec((ns, CONV_C_WIDTH - 1, c), lambda i, j: (i, 0, 0))],
        out_shape=[jax.ShapeDtypeStruct((nseq, t_len, c), F32),
                   jax.ShapeDtypeStruct((nseq, CONV_C_WIDTH - 1, c), F32)],
        scratch_shapes=[pltpu.VMEM((ns, CONV_C_PAD + tb, c), F32)],
        compiler_params=_params("parallel", "arbitrary"),
        name="conv_c",
    )(bg, cv, prev, w)


def _s5prep_kernel(lr_ref, li_ref, ldt_ref, br_ref, bi_ref, ar_ref, ai_ref, bbr_ref, bbi_ref):
    lr, li = lr_ref[...], li_ref[...]
    dt = jnp.exp(ldt_ref[...])
    mag = jnp.exp(lr * dt)
    ar, ai = mag * jnp.cos(li * dt), mag * jnp.sin(li * dt)
    den = lr * lr + li * li
    qr = ((ar - 1.0) * lr + ai * li) / den
    qi = (ai * lr - (ar - 1.0) * li) / den
    br, bi = br_ref[...], bi_ref[...]
    ar_ref[...] = ar
    ai_ref[...] = ai
    bbr_ref[...] = qr * br - qi * bi
    bbi_ref[...] = qr * bi + qi * br


def _s5_prep(lam_re, lam_im, log_dt, b_re, b_im):
    g, n = lam_re.shape
    h = b_re.shape[-1]
    lam3 = lambda a: a.reshape(g, 1, n)
    ldt = jnp.broadcast_to(log_dt.reshape(g, 1, 1), (g, 1, n))
    bt = lambda a: jnp.transpose(a, (0, 2, 1))
    small = jax.ShapeDtypeStruct((g, 1, n), F32)
    big = jax.ShapeDtypeStruct((g, h, n), F32)
    return pl.pallas_call(_s5prep_kernel, out_shape=[small, small, big, big], name="s5_prep")(
        lam3(lam_re), lam3(lam_im), ldt, bt(b_re), bt(b_im))


def _block_diag_tiles(x):
    g, r, c = x.shape
    tg = S5_TILE_GROUPS
    eye = jnp.eye(tg, dtype=x.dtype)
    y = x.reshape(g // tg, tg, r, 1, c) * eye[None, :, None, :, None]
    return y.reshape(g // tg, tg * r, tg * c)


def _s5_kernel(u_ref, h0r_ref, h0i_ref, ar_ref, ai_ref, bre_ref, bim_ref, cre_ref, cim_ref, d_ref, wglu_ref,
               bglu_ref, y_ref, hr_ref, hi_ref, bur_ref, bui_ref, *, ns, tb, sb):
    n_tiles, k_in, k_st = bre_ref.shape
    rows = ns * tb
    lb = k_st // LANES

    @pl.when(pl.program_id(1) == 0)
    def _():
        hr_ref[...] = h0r_ref[...]
        hi_ref[...] = h0i_ref[...]

    u = u_ref[...].reshape(rows, n_tiles * k_in)
    ub = u.astype(BF16)
    for k in range(n_tiles):
        uk = ub[:, k * k_in:(k + 1) * k_in]
        bur = _dot(uk, bre_ref[k])
        bui = _dot(uk, bim_ref[k])
        for l in range(lb):
            bur_ref[k * lb + l] = bur[:, l * LANES:(l + 1) * LANES]
            bui_ref[k * lb + l] = bui[:, l * LANES:(l + 1) * LANES]

    for q in range(ns // sb):
        seqs = slice(q * sb, (q + 1) * sb)
        for k in range(n_tiles):
            blocks = slice(k * lb, (k + 1) * lb)
            ar = jnp.broadcast_to(ar_ref[blocks], (lb, sb, LANES))
            ai = jnp.broadcast_to(ai_ref[blocks], (lb, sb, LANES))

            def step(t, carry, q=q, blocks=blocks, ar=ar, ai=ai):
                hr, hi = carry
                at = pl.ds(q * sb * tb + t, sb, stride=tb)
                nr = ar * hr - ai * hi + bur_ref[blocks, at, :]
                ni = ar * hi + ai * hr + bui_ref[blocks, at, :]
                bur_ref[blocks, at, :] = nr
                bui_ref[blocks, at, :] = ni
                return nr, ni

            hr, hi = lax.fori_loop(0, tb, step, (hr_ref[blocks, seqs, :], hi_ref[blocks, seqs, :]),
                                   unroll=min(tb, 8))
            hr_ref[blocks, seqs, :] = hr
            hi_ref[blocks, seqs, :] = hi

    ys = []
    for k in range(n_tiles):
        hrk = jnp.concatenate([bur_ref[k * lb + l] for l in range(lb)], axis=-1).astype(BF16)
        hik = jnp.concatenate([bui_ref[k * lb + l] for l in range(lb)], axis=-1).astype(BF16)
        ys.append(_dot(hrk, cre_ref[k]) - _dot(hik, cim_ref[k]))
    y = jnp.concatenate(ys, axis=-1) + d_ref[...] * u
    y = jax.nn.gelu(y)
    y = y * jax.nn.sigmoid(_dot(y.astype(BF16), wglu_ref[...]) + bglu_ref[...])
    y_ref[...] = y.reshape(y_ref.shape)


def _s5(u, h0r, h0i, ar, ai, bre, bim, cre, cim, d, wglu, bglu, *, ns, tb, sb):
    nseq, t_len, c = u.shape
    n_state = h0r.shape[1]
    nlb = n_state // LANES
    to_blocks = lambda a: jnp.transpose(a.reshape(a.shape[0], nlb, LANES), (1, 0, 2))
    from_blocks = lambda a: jnp.transpose(a, (1, 0, 2)).reshape(a.shape[1], n_state)
    ar, ai = to_blocks(ar), to_blocks(ai)
    blk = pl.BlockSpec((ns, tb, c), lambda i, j: (i, j, 0))
    st = pl.BlockSpec((nlb, ns, LANES), lambda i, j: (0, i, 0))
    st_shape = jax.ShapeDtypeStruct((nlb, nseq, LANES), F32)
    y, hr, hi = pl.pallas_call(
        functools.partial(_s5_kernel, ns=ns, tb=tb, sb=sb),
        grid=(nseq // ns, t_len // tb),
        in_specs=[blk, st, st, _resident(ar.shape), _resident(ai.shape), _resident(bre.shape),
                  _resident(bim.shape), _resident(cre.shape), _resident(cim.shape), _resident(d.shape),
                  _resident(wglu.shape), _resident(bglu.shape)],
        out_specs=[blk, st, st],
        out_shape=[jax.ShapeDtypeStruct((nseq, t_len, c), F32), st_shape, st_shape],
        scratch_shapes=[pltpu.VMEM((nlb, ns * tb, LANES), F32), pltpu.VMEM((nlb, ns * tb, LANES), F32)],
        compiler_params=_params("parallel", "arbitrary"),
        name="s5",
    )(u, to_blocks(h0r), to_blocks(h0i), ar, ai, bre, bim, cre, cim, d, wglu, bglu)
    return y, from_blocks(hr), from_blocks(hi)


def _trunk(x, p, conv_a, s5_re, s5_im, conv_c, w, cfg):
    nseq, t_len, d = x.shape
    m = nseq * t_len
    depth = p.shape[0]
    flat = lambda a: a.reshape(m, a.shape[-1])
    seq = lambda a: a.reshape(nseq, t_len, a.shape[-1])
    x = flat(x)
    p = p.reshape(depth, m, p.shape[-1])
    new_ca, new_sr, new_si, new_cc = [], [], [], []
    for i in range(depth):
        j = i // 2
        h = _ffn(x, w["norm_ffn1"][i], w["w_ffn1_gate"][i], w["w_ffn1_up"][i], w["w_ffn1_down"][i])
        if i % 2 == 0:
            a, u = _in_ab(h, w["norm_mix"][i], w["w_in_ab"][j])
            a, ca = _conv_a(seq(a), conv_a[j], w["w_dw_a"][j], w["b_dw_a"][j], w["ln_a_g"][j], w["ln_a_b"][j],
                            **cfg["conv_a"])
            s5w = w["s5"][j]
            n_state = s5_re.shape[-2] * s5_re.shape[-1]
            yb, sr, si = _s5(seq(u), s5_re[j].reshape(nseq, n_state), s5_im[j].reshape(nseq, n_state), *s5w,
                             w["w_glu_b"][j], w["b_glu_b"][j], **cfg["s5"])
            h = _out_proj([flat(a), flat(yb)], w["w_out_ab"][j], h)
            new_ca.append(ca)
            new_sr.append(sr.reshape(s5_re.shape[1:]))
            new_si.append(si.reshape(s5_im.shape[1:]))
        else:
            bg, cv = _in_c(h, w["norm_mix"][i], w["w_in_c"][j])
            mix, cc = _conv_c(seq(bg), seq(cv), conv_c[j], w["w_dw_c"][j], **cfg["conv_c"])
            h = _out_proj([flat(mix)], w["w_out_c"][j], h)
            new_cc.append(cc)
        h = _ffn(h, w["norm_ffn2"][i], w["w_ffn2_gate"][i], w["w_ffn2_up"][i], w["w_ffn2_down"][i])
        x = _ple(h, p, i, w["norm_ple"][i], w["w_ple_gate"][i], w["w_ple_proj"][i],
                 w["norm_final"] if i == depth - 1 else None)
    return (x.reshape(nseq, t_len, d), jnp.stack(new_ca), jnp.stack(new_sr), jnp.stack(new_si),
            jnp.stack(new_cc))


def _prepare_weights(norm_ffn1, w_ffn1_gate, w_ffn1_up, w_ffn1_down, norm_mix, norm_ffn2, w_ffn2_gate, w_ffn2_up,
                     w_ffn2_down, norm_ple, w_ple_gate, w_ple_proj, w_in_ab, w_dw_a, b_dw_a, ln_a_g, ln_a_b,
                     s5_lam_re, s5_lam_im, s5_log_dt, s5_b_re, s5_b_im, s5_c_re, s5_c_im, s5_d, w_glu_b, b_glu_b,
                     w_out_ab, w_in_c, w_dw_c, w_out_c, norm_final):
    row = lambda a: a[..., None, :]
    bf = lambda a: a.astype(BF16)
    w = dict(
        norm_ffn1=row(norm_ffn1), norm_mix=row(norm_mix), norm_ffn2=row(norm_ffn2), norm_ple=row(norm_ple),
        norm_final=norm_final[None, :],
        w_ffn1_gate=bf(w_ffn1_gate), w_ffn1_up=bf(w_ffn1_up), w_ffn1_down=bf(w_ffn1_down),
        w_ffn2_gate=bf(w_ffn2_gate), w_ffn2_up=bf(w_ffn2_up), w_ffn2_down=bf(w_ffn2_down),
        w_ple_gate=bf(w_ple_gate), w_ple_proj=bf(w_ple_proj),
        w_in_ab=bf(w_in_ab), w_out_ab=bf(w_out_ab), w_glu_b=bf(w_glu_b), b_glu_b=row(b_glu_b),
        w_dw_a=w_dw_a, b_dw_a=row(b_dw_a), ln_a_g=row(ln_a_g), ln_a_b=row(ln_a_b),
        w_in_c=bf(w_in_c), w_dw_c=w_dw_c, w_out_c=bf(w_out_c),
    )
    s5 = []
    for j in range(s5_lam_re.shape[0]):
        ar, ai, bbr, bbi = _s5_prep(s5_lam_re[j], s5_lam_im[j], s5_log_dt[j], s5_b_re[j], s5_b_im[j])
        ct = lambda a: jnp.transpose(a, (0, 2, 1))
        s5.append((ar.reshape(1, -1), ai.reshape(1, -1),
                   bf(_block_diag_tiles(bbr)), bf(_block_diag_tiles(bbi)),
                   bf(_block_diag_tiles(ct(s5_c_re[j]))), bf(_block_diag_tiles(ct(s5_c_im[j]))),
                   s5_d[j][None, :]))
    w["s5"] = s5
    return w


def kernel(x_prompt, x_sample, p_prompt, p_sample, state_convA, state_s5_re, state_s5_im, state_convC, norm_ffn1, w_ffn1_gate, w_ffn1_up, w_ffn1_down, norm_mix, norm_ffn2, w_ffn2_gate, w_ffn2_up, w_ffn2_down, norm_ple, w_ple_gate, w_ple_proj, w_in_ab, w_dw_a, b_dw_a, ln_a_g, ln_a_b, s5_lam_re, s5_lam_im, s5_log_dt, s5_b_re, s5_b_im, s5_c_re, s5_c_im, s5_d, w_glu_b, b_glu_b, w_out_ab, w_in_c, w_dw_c, w_out_c, norm_final):
    w = _prepare_weights(norm_ffn1, w_ffn1_gate, w_ffn1_up, w_ffn1_down, norm_mix, norm_ffn2, w_ffn2_gate,
                         w_ffn2_up, w_ffn2_down, norm_ple, w_ple_gate, w_ple_proj, w_in_ab, w_dw_a, b_dw_a,
                         ln_a_g, ln_a_b, s5_lam_re, s5_lam_im, s5_log_dt, s5_b_re, s5_b_im, s5_c_re, s5_c_im,
                         s5_d, w_glu_b, b_glu_b, w_out_ab, w_in_c, w_dw_c, w_out_c, norm_final)
    bp = x_prompt.shape[0]
    n_even, n_odd = state_convA.shape[0], state_convC.shape[0]
    ca0 = jnp.zeros((n_even, bp) + state_convA.shape[2:], F32)
    sr0 = jnp.zeros((n_even, bp) + state_s5_re.shape[2:], F32)
    si0 = jnp.zeros((n_even, bp) + state_s5_im.shape[2:], F32)
    cc0 = jnp.zeros((n_odd, bp) + state_convC.shape[2:], F32)
    cfg_prompt = dict(conv_a=dict(ns=1, tb=256, rows=32), conv_c=dict(ns=1, tb=256, rows=16),
                      s5=dict(ns=bp, tb=64, sb=bp))
    t_s = x_sample.shape[1]
    cfg_sample = dict(conv_a=dict(ns=16, tb=t_s, rows=t_s), conv_c=dict(ns=16, tb=t_s, rows=t_s),
                      s5=dict(ns=32, tb=t_s, sb=SUBLANES))
    y_p, ca_p, sr_p, si_p, cc_p = _trunk(x_prompt, p_prompt, ca0, sr0, si0, cc0, w, cfg_prompt)
    y_s, ca_s, sr_s, si_s, cc_s = _trunk(x_sample, p_sample, state_convA, state_s5_re, state_s5_im, state_convC, w,
                                         cfg_sample)
    return (y_p, y_s, ca_p, ca_s, sr_p, si_p, sr_s, si_s, cc_p, cc_s)
```

```python
import functools

import jax
import jax.numpy as jnp
from jax import lax
from jax.experimental import pallas as pl
from jax.experimental.pallas import tpu as pltpu

F32 = jnp.float32
BF16 = jnp.bfloat16
EPS = 1e-6

LANES = 128
SUBLANES = 8
VMEM_LIMIT = 56 * 1024 * 1024

S5_GROUP = 16
S5_STATE = 64
S5_TILE_GROUPS = 16
SCAN_SHIFTS = (1, 2, 4)
S5_SCAN_SEQS = 4
S5_SCAN_LANES = 2 * LANES
CONV_A_WIDTH = 31
CONV_C_WIDTH = 3
CONV_A_PAD = 32
CONV_C_PAD = 8
CONV_SHIFT_ROWS = 40


def _dot(a, b):
    return jnp.dot(a, b, preferred_element_type=F32)


def _rmsnorm_bf16(x, g):
    ms = jnp.mean(x * x, axis=-1, keepdims=True)
    return (x * lax.rsqrt(ms + EPS) * g).astype(BF16)


def _params(*sem):
    return pltpu.CompilerParams(dimension_semantics=sem, vmem_limit_bytes=VMEM_LIMIT)


def _resident(shape):
    nd = len(shape)
    return pl.BlockSpec(shape, lambda *_: (0,) * nd, pipeline_mode=pl.Buffered(1))


def _ffn_kernel(x_ref, g_ref, wg_ref, wu_ref, wd_ref, o_ref, xn_ref):
    @pl.when(pl.program_id(1) == 0)
    def _():
        x = x_ref[...]
        xn_ref[...] = _rmsnorm_bf16(x, g_ref[...])
        o_ref[...] = x

    xn = xn_ref[...]
    gate = _dot(xn, wg_ref[...].astype(BF16))
    up = _dot(xn, wu_ref[...].astype(BF16))
    act = (gate * jax.nn.sigmoid(gate) * (0.5 * up)).astype(BF16)
    o_ref[...] += _dot(act, wd_ref[...].astype(BF16))


def _ffn(x, g, wg, wu, wd, layer, *, tm=1024, tf=256):
    m, d = x.shape
    f = wg.shape[-1]
    return pl.pallas_call(
        _ffn_kernel,
        grid=(m // tm, f // tf),
        in_specs=[
            pl.BlockSpec((tm, d), lambda i, j: (i, 0), pipeline_mode=pl.Buffered(1)),
            pl.BlockSpec((1, d), lambda i, j: (0, 0)),
            pl.BlockSpec((None, d, tf), lambda i, j: (layer, 0, j)),
            pl.BlockSpec((None, d, tf), lambda i, j: (layer, 0, j)),
            pl.BlockSpec((None, tf, d), lambda i, j: (layer, j, 0)),
        ],
        out_specs=pl.BlockSpec((tm, d), lambda i, j: (i, 0)),
        out_shape=jax.ShapeDtypeStruct((m, d), F32),
        scratch_shapes=[pltpu.VMEM((tm, d), BF16)],
        compiler_params=_params("parallel", "arbitrary"),
        name="ffn",
    )(x, g, wg, wu, wd)


def _inab_kernel(x_ref, g_ref, wv_ref, wgt_ref, wu_ref, a_ref, u_ref, xn_ref):
    @pl.when(pl.program_id(1) == 0)
    def _():
        xn_ref[...] = _rmsnorm_bf16(x_ref[...], g_ref[...])

    xn = xn_ref[...]
    a_ref[...] = _dot(xn, wv_ref[...]) * jax.nn.sigmoid(_dot(xn, wgt_ref[...]))
    u_ref[...] = _dot(xn, wu_ref[...])


def _in_ab(x, g, w, *, tm=1024, tn=512):
    m, d = x.shape
    c = w.shape[1] // 3
    nb = c // tn
    wspec = lambda k: pl.BlockSpec((d, tn), lambda i, j: (0, j + k * nb))
    ospec = pl.BlockSpec((tm, tn), lambda i, j: (i, j))
    return pl.pallas_call(
        _inab_kernel,
        grid=(m // tm, nb),
        in_specs=[pl.BlockSpec((tm, d), lambda i, j: (i, 0)), pl.BlockSpec((1, d), lambda i, j: (0, 0)),
                  wspec(0), wspec(1), wspec(2)],
        out_specs=[ospec, ospec],
        out_shape=[jax.ShapeDtypeStruct((m, c), F32)] * 2,
        scratch_shapes=[pltpu.VMEM((tm, d), BF16)],
        compiler_params=_params("parallel", "arbitrary"),
        name="in_ab",
    )(x, g, w, w, w)


def _inc_kernel(x_ref, g_ref, wb_ref, wc_ref, wv_ref, bg_ref, cv_ref, xn_ref):
    @pl.when(pl.program_id(1) == 0)
    def _():
        xn_ref[...] = _rmsnorm_bf16(x_ref[...], g_ref[...])

    xn = xn_ref[...]
    bg_ref[...] = _dot(xn, wb_ref[...])
    cv_ref[...] = _dot(xn, wc_ref[...]) * _dot(xn, wv_ref[...])


def _in_c(x, g, w, *, tm=1024, tn=512):
    m, d = x.shape
    c = w.shape[1] // 3
    nb = c // tn
    wspec = lambda k: pl.BlockSpec((d, tn), lambda i, j: (0, j + k * nb))
    ospec = pl.BlockSpec((tm, tn), lambda i, j: (i, j))
    return pl.pallas_call(
        _inc_kernel,
        grid=(m // tm, nb),
        in_specs=[pl.BlockSpec((tm, d), lambda i, j: (i, 0)), pl.BlockSpec((1, d), lambda i, j: (0, 0)),
                  wspec(0), wspec(1), wspec(2)],
        out_specs=[ospec, ospec],
        out_shape=[jax.ShapeDtypeStruct((m, c), F32)] * 2,
        scratch_shapes=[pltpu.VMEM((tm, d), BF16)],
        compiler_params=_params("parallel", "arbitrary"),
        name="in_c",
    )(x, g, w, w, w)


def _outproj_kernel(*refs, ksizes, tn):
    n = len(ksizes)
    act_refs, (w_ref, h_ref, o_ref) = refs[:n], refs[n:]
    acts = [a[...].astype(BF16) for a in act_refs]
    for c in range(o_ref.shape[1] // tn):
        cols = slice(c * tn, (c + 1) * tn)
        acc = h_ref[:, cols]
        off = 0
        for a, k in zip(acts, ksizes):
            acc = acc + _dot(a, w_ref[off:off + k, cols])
            off += k
        o_ref[:, cols] = acc


def _out_proj(acts, w, h, *, tm=512, tn=512):
    m, d = h.shape
    ksizes = tuple(a.shape[1] for a in acts)
    row = lambda k: pl.BlockSpec((tm, k), lambda i: (i, 0))
    return pl.pallas_call(
        functools.partial(_outproj_kernel, ksizes=ksizes, tn=tn),
        grid=(m // tm,),
        in_specs=[row(k) for k in ksizes] + [_resident(w.shape), row(d)],
        out_specs=row(d),
        out_shape=jax.ShapeDtypeStruct((m, d), F32),
        compiler_params=_params("parallel"),
        name="out_proj",
    )(*acts, w, h)


def _ple_kernel(*refs, final, tn):
    if final:
        x_ref, p_ref, g_ref, wg_ref, wp_ref, gf_ref, o_ref = refs
    else:
        x_ref, p_ref, g_ref, wg_ref, wp_ref, o_ref = refs
    d = o_ref.shape[1]
    xn = _rmsnorm_bf16(x_ref[...], g_ref[...])
    pb = p_ref[...].astype(BF16)
    ssq = jnp.zeros((o_ref.shape[0], 1), F32)
    for c in range(d // tn):
        cols = slice(c * tn, (c + 1) * tn)
        gate = jax.nn.sigmoid(_dot(xn, wg_ref[:, cols]))
        y = x_ref[:, cols] + gate * _dot(pb, wp_ref[:, cols])
        ssq = ssq + jnp.sum(y * y, axis=-1, keepdims=True)
        o_ref[:, cols] = y
    if final:
        o_ref[...] = o_ref[...] * lax.rsqrt(ssq / d + EPS) * gf_ref[...]


def _ple(x, p, layer, g, wg, wp, gf, *, tm=512, tn=512):
    m, d = x.shape
    dp = p.shape[-1]
    final = gf is not None
    in_specs = [
        pl.BlockSpec((tm, d), lambda i: (i, 0)),
        pl.BlockSpec((None, tm, dp), lambda i: (layer, i, 0)),
        _resident((1, d)),
        _resident(wg.shape),
        _resident(wp.shape),
    ]
    args = [x, p, g, wg, wp]
    if final:
        in_specs.append(_resident((1, d)))
        args.append(gf)
    return pl.pallas_call(
        functools.partial(_ple_kernel, final=final, tn=tn),
        grid=(m // tm,),
        in_specs=in_specs,
        out_specs=pl.BlockSpec((tm, d), lambda i: (i, 0)),
        out_shape=jax.ShapeDtypeStruct((m, d), F32),
        compiler_params=_params("parallel"),
        name="ple",
    )(*args)


def _conva_kernel(x_ref, prev_ref, w_ref, b_ref, g_ref, beta_ref, o_ref, st_ref, buf_ref, *shift_refs, ns, tb, nt, rows):
    t = pl.program_id(1)
    pad, k_w = CONV_A_PAD, CONV_A_WIDTH
    first = pad - (k_w - 1)

    @pl.when(t == 0)
    def _():
        buf_ref[:, 0:pad, :] = prev_ref[...]

    buf_ref[:, pad:pad + tb, :] = x_ref[...]

    def tap(s, r, k):
        off = first + k
        if not shift_refs:
            return buf_ref[s, pl.ds(r * rows + off, rows), :]
        j = off % SUBLANES
        if j == 0:
            return buf_ref[s, pl.ds(r * rows + off, rows), :]
        return shift_refs[0][j - 1, pl.ds(r * rows + off - j, rows), :]

    def per_seq(s, carry):
        if shift_refs:
            span = shift_refs[0].shape[1]
            for j in range(1, SUBLANES):
                for r0 in range(0, span, CONV_SHIFT_ROWS):
                    shift_refs[0][j - 1, r0:r0 + CONV_SHIFT_ROWS, :] = buf_ref[s, pl.ds(r0 + j, CONV_SHIFT_ROWS), :]
        for r in range(tb // rows):
            acc = None
            for k in range(k_w):
                term = tap(s, r, k) * w_ref[k:k + 1, :]
                acc = term if acc is None else acc + term
            y = acc + b_ref[...]
            mu = jnp.mean(y, axis=-1, keepdims=True)
            yc = y - mu
            var = jnp.mean(yc * yc, axis=-1, keepdims=True)
            z = yc * lax.rsqrt(var + EPS) * g_ref[...] + beta_ref[...]
            o_ref[s, r * rows:(r + 1) * rows, :] = z * jax.nn.sigmoid(z)
        return carry

    lax.fori_loop(0, ns, per_seq, 0)

    @pl.when(t == nt - 1)
    def _():
        st_ref[...] = buf_ref[:, tb + first:tb + pad, :]

    if nt > 1:
        buf_ref[:, 0:pad, :] = buf_ref[:, tb:tb + pad, :]


def _conv_a(x, prev, w, b, g, beta, *, ns, tb, rows, shift):
    nseq, t_len, c = x.shape
    nt = t_len // tb
    prev = jnp.pad(prev, ((0, 0), (CONV_A_PAD - (CONV_A_WIDTH - 1), 0), (0, 0)))
    scratch = [pltpu.VMEM((ns, CONV_A_PAD + tb, c), F32)]
    if shift:
        span = tb + CONV_A_PAD - SUBLANES
        assert span % CONV_SHIFT_ROWS == 0
        scratch.append(pltpu.VMEM((SUBLANES - 1, span, c), F32))
    return pl.pallas_call(
        functools.partial(_conva_kernel, ns=ns, tb=tb, nt=nt, rows=rows),
        grid=(nseq // ns, nt),
        in_specs=[
            pl.BlockSpec((ns, tb, c), lambda i, j: (i, j, 0)),
            pl.BlockSpec((ns, CONV_A_PAD, c), lambda i, j: (i, 0, 0)),
            pl.BlockSpec((CONV_A_WIDTH, c), lambda i, j: (0, 0)),
            pl.BlockSpec((1, c), lambda i, j: (0, 0)),
            pl.BlockSpec((1, c), lambda i, j: (0, 0)),
            pl.BlockSpec((1, c), lambda i, j: (0, 0)),
        ],
        out_specs=[
            pl.BlockSpec((ns, tb, c), lambda i, j: (i, j, 0)),
            pl.BlockSpec((ns, CONV_A_WIDTH - 1, c), lambda i, j: (i, 0, 0)),
        ],
        out_shape=[jax.ShapeDtypeStruct((nseq, t_len, c), F32),
                   jax.ShapeDtypeStruct((nseq, CONV_A_WIDTH - 1, c), F32)],
        scratch_shapes=scratch,
        compiler_params=_params("parallel", "arbitrary"),
        name="conv_a",
    )(x, prev, w, b, g, beta)


def _convc_kernel(bg_ref, cv_ref, prev_ref, w_ref, o_ref, st_ref, buf_ref, *, ns, tb, nt, rows):
    t = pl.program_id(1)
    pad, k_w = CONV_C_PAD, CONV_C_WIDTH
    first = pad - (k_w - 1)

    @pl.when(t == 0)
    def _():
        buf_ref[:, 0:pad, :] = prev_ref[...]

    buf_ref[:, pad:pad + tb, :] = cv_ref[...]

    def per_seq(s, carry):
        for r in range(tb // rows):
            acc = None
            for k in range(k_w):
                term = buf_ref[s, pl.ds(r * rows + first + k, rows), :] * w_ref[k:k + 1, :]
                acc = term if acc is None else acc + term
            o_ref[s, r * rows:(r + 1) * rows, :] = bg_ref[s, r * rows:(r + 1) * rows, :] * acc
        return carry

    lax.fori_loop(0, ns, per_seq, 0)

    @pl.when(t == nt - 1)
    def _():
        st_ref[...] = buf_ref[:, tb + first:tb + pad, :]

    if nt > 1:
        buf_ref[:, 0:pad, :] = buf_ref[:, tb:tb + pad, :]


def _conv_c(bg, cv, prev, w, *, ns, tb, rows):
    nseq, t_len, c = cv.shape
    nt = t_len // tb
    prev = jnp.pad(prev, ((0, 0), (CONV_C_PAD - (CONV_C_WIDTH - 1), 0), (0, 0)))
    blk = pl.BlockSpec((ns, tb, c), lambda i, j: (i, j, 0))
    return pl.pallas_call(
        functools.partial(_convc_kernel, ns=ns, tb=tb, nt=nt, rows=rows),
        grid=(nseq // ns, nt),
        in_specs=[blk, blk,
                  pl.BlockSpec((ns, CONV_C_PAD, c), lambda i, j: (i, 0, 0)),
                  pl.BlockSpec((CONV_C_WIDTH, c), lambda i, j: (0, 0))],
        out_specs=[blk, pl.BlockSpec((ns, CONV_C_WIDTH - 1, c), lambda i, j: (i, 0, 0))],
        out_shape=[jax.ShapeDtypeStruct((nseq, t_len, c), F32),
                   jax.ShapeDtypeStruct((nseq, CONV_C_WIDTH - 1, c), F32)],
        scratch_shapes=[pltpu.VMEM((ns, CONV_C_PAD + tb, c), F32)],
        compiler_params=_params("parallel", "arbitrary"),
        name="conv_c",
    )(bg, cv, prev, w)


def _s5prep_kernel(lr_ref, li_ref, ldt_ref, br_ref, bi_ref, pwr_ref, pwi_ref, bbr_ref, bbi_ref):
    lr, li = lr_ref[...], li_ref[...]
    dt = jnp.exp(ldt_ref[...])
    mag = jnp.exp(lr * dt)
    ar, ai = mag * jnp.cos(li * dt), mag * jnp.sin(li * dt)
    den = lr * lr + li * li
    qr = ((ar - 1.0) * lr + ai * li) / den
    qi = (ai * lr - (ar - 1.0) * li) / den
    br, bi = br_ref[...], bi_ref[...]
    bbr_ref[...] = qr * br - qi * bi
    bbi_ref[...] = qr * bi + qi * br
    pr, pi = ar, ai
    for r in range(SUBLANES):
        pwr_ref[:, r:r + 1, :] = pr
        pwi_ref[:, r:r + 1, :] = pi
        pr, pi = pr * ar - pi * ai, pr * ai + pi * ar


def _s5_prep(lam_re, lam_im, log_dt, b_re, b_im):
    g, n = lam_re.shape
    h = b_re.shape[-1]
    lam3 = lambda a: a.reshape(g, 1, n)
    ldt = jnp.broadcast_to(log_dt.reshape(g, 1, 1), (g, 1, n))
    bt = lambda a: jnp.transpose(a, (0, 2, 1))
    small = jax.ShapeDtypeStruct((g, SUBLANES, n), F32)
    big = jax.ShapeDtypeStruct((g, h, n), F32)
    return pl.pallas_call(_s5prep_kernel, out_shape=[small, small, big, big], name="s5_prep")(
        lam3(lam_re), lam3(lam_im), ldt, bt(b_re), bt(b_im))


def _scan_tiles(pw):
    g, r, n = pw.shape
    carry = jnp.transpose(pw, (1, 0, 2)).reshape(r, g * n)
    row = jnp.arange(r)[:, None]
    levels = [jnp.where(row >= sh, carry[sh - 1:sh, :], 0.0) for sh in SCAN_SHIFTS]
    return carry, jnp.stack(levels)


def _block_diag_tiles(x):
    g, r, c = x.shape
    tg = S5_TILE_GROUPS
    eye = jnp.eye(tg, dtype=x.dtype)
    y = x.reshape(g // tg, tg, r, 1, c) * eye[None, :, None, :, None]
    return y.reshape(g // tg, tg * r, tg * c)


def _s5_kernel(u_ref, h0r_ref, h0i_ref, pcr_ref, pci_ref, plr_ref, pli_ref, bre_ref, bim_ref, cre_ref, cim_ref, d_ref,
               wglu_ref, bglu_ref, y_ref, hr_ref, hi_ref, bur_ref, bui_ref, *, ns, tb):
    n_tiles, k_in, k_st = bre_ref.shape
    rows = ns * tb
    n_state = n_tiles * k_st
    width = S5_SCAN_LANES
    t_tiles = tb // SUBLANES

    @pl.when(pl.program_id(1) == 0)
    def _():
        hr_ref[...] = h0r_ref[...]
        hi_ref[...] = h0i_ref[...]

    u = u_ref[...].reshape(rows, n_tiles * k_in)
    ub = u.astype(BF16)
    for k in range(n_tiles):
        uk = ub[:, k * k_in:(k + 1) * k_in]
        bur_ref[:, k * k_st:(k + 1) * k_st] = _dot(uk, bre_ref[k])
        bui_ref[:, k * k_st:(k + 1) * k_st] = _dot(uk, bim_ref[k])

    def scan_group(grp):
        for c in range(n_state // width):
            lanes = slice(c * width, (c + 1) * width)

            def tile(i, carry, lanes=lanes):
                out = []
                for s in range(S5_SCAN_SEQS):
                    hr, hi = carry[2 * s], carry[2 * s + 1]
                    at = pl.ds(pl.multiple_of((grp * S5_SCAN_SEQS + s) * tb + i * SUBLANES, SUBLANES), SUBLANES)
                    xr, xi = bur_ref[at, lanes], bui_ref[at, lanes]
                    for k, sh in enumerate(SCAN_SHIFTS):
                        sr, si = pltpu.roll(xr, sh, axis=0), pltpu.roll(xi, sh, axis=0)
                        ar, ai = plr_ref[k, :, lanes], pli_ref[k, :, lanes]
                        xr, xi = xr + (ar * sr - ai * si), xi + (ar * si + ai * sr)
                    cr, ci = jnp.broadcast_to(hr, xr.shape), jnp.broadcast_to(hi, xi.shape)
                    pr, pi = pcr_ref[:, lanes], pci_ref[:, lanes]
                    xr, xi = xr + (pr * cr - pi * ci), xi + (pr * ci + pi * cr)
                    bur_ref[at, lanes] = xr
                    bui_ref[at, lanes] = xi
                    out += [xr[SUBLANES - 1:, :], xi[SUBLANES - 1:, :]]
                return tuple(out)

            init = []
            for s in range(S5_SCAN_SEQS):
                seq = pl.ds(grp * S5_SCAN_SEQS + s, 1)
                init += [hr_ref[seq, lanes], hi_ref[seq, lanes]]
            last = tile(0, tuple(init)) if t_tiles == 1 else lax.fori_loop(0, t_tiles, tile, tuple(init))
            for s in range(S5_SCAN_SEQS):
                seq = pl.ds(grp * S5_SCAN_SEQS + s, 1)
                hr_ref[seq, lanes] = last[2 * s]
                hi_ref[seq, lanes] = last[2 * s + 1]

    n_groups = ns // S5_SCAN_SEQS
    if n_groups == 1:
        scan_group(0)
    else:
        def group_step(grp, carry):
            scan_group(grp)
            return carry
        lax.fori_loop(0, n_groups, group_step, 0)

    ys = []
    for k in range(n_tiles):
        st = slice(k * k_st, (k + 1) * k_st)
        ys.append(_dot(bur_ref[:, st].astype(BF16), cre_ref[k]) - _dot(bui_ref[:, st].astype(BF16), cim_ref[k]))
    y = jnp.concatenate(ys, axis=-1) + d_ref[...] * u
    y = jax.nn.gelu(y)
    y = y * jax.nn.sigmoid(_dot(y.astype(BF16), wglu_ref[...]) + bglu_ref[...])
    y_ref[...] = y.reshape(y_ref.shape)


def _s5(u, h0r, h0i, pcr, pci, plr, pli, bre, bim, cre, cim, d, wglu, bglu, *, ns, tb):
    nseq, t_len, c = u.shape
    n_state = h0r.shape[1]
    blk = pl.BlockSpec((ns, tb, c), lambda i, j: (i, j, 0))
    st = pl.BlockSpec((ns, n_state), lambda i, j: (i, 0))
    st_shape = jax.ShapeDtypeStruct((nseq, n_state), F32)
    consts = (pcr, pci, plr, pli, bre, bim, cre, cim, d, wglu, bglu)
    return pl.pallas_call(
        functools.partial(_s5_kernel, ns=ns, tb=tb),
        grid=(nseq // ns, t_len // tb),
        in_specs=[blk, st, st] + [_resident(a.shape) for a in consts],
        out_specs=[blk, st, st],
        out_shape=[jax.ShapeDtypeStruct((nseq, t_len, c), F32), st_shape, st_shape],
        scratch_shapes=[pltpu.VMEM((ns * tb, n_state), F32), pltpu.VMEM((ns * tb, n_state), F32)],
        compiler_params=_params("parallel", "arbitrary"),
        name="s5",
    )(u, h0r, h0i, *consts)


def _trunk(x, p, conv_a, s5_re, s5_im, conv_c, w, cfg):
    nseq, t_len, d = x.shape
    m = nseq * t_len
    depth = p.shape[0]
    flat = lambda a: a.reshape(m, a.shape[-1])
    seq = lambda a: a.reshape(nseq, t_len, a.shape[-1])
    x = flat(x)
    p = p.reshape(depth, m, p.shape[-1])
    new_ca, new_sr, new_si, new_cc = [], [], [], []
    for i in range(depth):
        j = i // 2
        h = _ffn(x, w["norm_ffn1"][i], w["w_ffn1_gate"], w["w_ffn1_up"], w["w_ffn1_down"], i)
        if i % 2 == 0:
            a, u = _in_ab(h, w["norm_mix"][i], w["w_in_ab"][j])
            a, ca = _conv_a(seq(a), conv_a[j], w["w_dw_a"][j], w["b_dw_a"][j], w["ln_a_g"][j], w["ln_a_b"][j],
                            **cfg["conv_a"])
            s5w = w["s5"][j]
            n_state = s5_re.shape[-2] * s5_re.shape[-1]
            yb, sr, si = _s5(seq(u), s5_re[j].reshape(nseq, n_state), s5_im[j].reshape(nseq, n_state), *s5w,
                             w["w_glu_b"][j], w["b_glu_b"][j], **cfg["s5"])
            h = _out_proj([flat(a), flat(yb)], w["w_out_ab"][j], h)
            new_ca.append(ca)
            new_sr.append(sr.reshape(s5_re.shape[1:]))
            new_si.append(si.reshape(s5_im.shape[1:]))
        else:
            bg, cv = _in_c(h, w["norm_mix"][i], w["w_in_c"][j])
            mix, cc = _conv_c(seq(bg), seq(cv), conv_c[j], w["w_dw_c"][j], **cfg["conv_c"])
            h = _out_proj([flat(mix)], w["w_out_c"][j], h)
            new_cc.append(cc)
        h = _ffn(h, w["norm_ffn2"][i], w["w_ffn2_gate"], w["w_ffn2_up"], w["w_ffn2_down"], i)
        x = _ple(h, p, i, w["norm_ple"][i], w["w_ple_gate"][i], w["w_ple_proj"][i],
                 w["norm_final"] if i == depth - 1 else None)
    return (x.reshape(nseq, t_len, d), jnp.stack(new_ca), jnp.stack(new_sr), jnp.stack(new_si),
            jnp.stack(new_cc))


def _prepare_weights(norm_ffn1, w_ffn1_gate, w_ffn1_up, w_ffn1_down, norm_mix, norm_ffn2, w_ffn2_gate, w_ffn2_up,
                     w_ffn2_down, norm_ple, w_ple_gate, w_ple_proj, w_in_ab, w_dw_a, b_dw_a, ln_a_g, ln_a_b,
                     s5_lam_re, s5_lam_im, s5_log_dt, s5_b_re, s5_b_im, s5_c_re, s5_c_im, s5_d, w_glu_b, b_glu_b,
                     w_out_ab, w_in_c, w_dw_c, w_out_c, norm_final):
    row = lambda a: a[..., None, :]
    bf = lambda a: a.astype(BF16)
    w = dict(
        norm_ffn1=row(norm_ffn1), norm_mix=row(norm_mix), norm_ffn2=row(norm_ffn2), norm_ple=row(norm_ple),
        norm_final=norm_final[None, :],
        w_ffn1_gate=w_ffn1_gate, w_ffn1_up=w_ffn1_up, w_ffn1_down=w_ffn1_down,
        w_ffn2_gate=w_ffn2_gate, w_ffn2_up=w_ffn2_up, w_ffn2_down=w_ffn2_down,
        w_ple_gate=bf(w_ple_gate), w_ple_proj=bf(w_ple_proj),
        w_in_ab=bf(w_in_ab), w_out_ab=bf(w_out_ab), w_glu_b=bf(w_glu_b), b_glu_b=row(b_glu_b),
        w_dw_a=w_dw_a, b_dw_a=row(b_dw_a), ln_a_g=row(ln_a_g), ln_a_b=row(ln_a_b),
        w_in_c=bf(w_in_c), w_dw_c=w_dw_c, w_out_c=bf(w_out_c),
    )
    s5 = []
    for j in range(s5_lam_re.shape[0]):
        pwr, pwi, bbr, bbi = _s5_prep(s5_lam_re[j], s5_lam_im[j], s5_log_dt[j], s5_b_re[j], s5_b_im[j])
        ct = lambda a: jnp.transpose(a, (0, 2, 1))
        pcr, plr = _scan_tiles(pwr)
        pci, pli = _scan_tiles(pwi)
        s5.append((pcr, pci, plr, pli,
                   bf(_block_diag_tiles(bbr)), bf(_block_diag_tiles(bbi)),
                   bf(_block_diag_tiles(ct(s5_c_re[j]))), bf(_block_diag_tiles(ct(s5_c_im[j]))),
                   s5_d[j][None, :]))
    w["s5"] = s5
    return w


def kernel(x_prompt, x_sample, p_prompt, p_sample, state_convA, state_s5_re, state_s5_im, state_convC, norm_ffn1, w_ffn1_gate, w_ffn1_up, w_ffn1_down, norm_mix, norm_ffn2, w_ffn2_gate, w_ffn2_up, w_ffn2_down, norm_ple, w_ple_gate, w_ple_proj, w_in_ab, w_dw_a, b_dw_a, ln_a_g, ln_a_b, s5_lam_re, s5_lam_im, s5_log_dt, s5_b_re, s5_b_im, s5_c_re, s5_c_im, s5_d, w_glu_b, b_glu_b, w_out_ab, w_in_c, w_dw_c, w_out_c, norm_final):
    w = _prepare_weights(norm_ffn1, w_ffn1_gate, w_ffn1_up, w_ffn1_down, norm_mix, norm_ffn2, w_ffn2_gate,
                         w_ffn2_up, w_ffn2_down, norm_ple, w_ple_gate, w_ple_proj, w_in_ab, w_dw_a, b_dw_a,
                         ln_a_g, ln_a_b, s5_lam_re, s5_lam_im, s5_log_dt, s5_b_re, s5_b_im, s5_c_re, s5_c_im,
                         s5_d, w_glu_b, b_glu_b, w_out_ab, w_in_c, w_dw_c, w_out_c, norm_final)
    bp = x_prompt.shape[0]
    n_even, n_odd = state_convA.shape[0], state_convC.shape[0]
    ca0 = jnp.zeros((n_even, bp) + state_convA.shape[2:], F32)
    sr0 = jnp.zeros((n_even, bp) + state_s5_re.shape[2:], F32)
    si0 = jnp.zeros((n_even, bp) + state_s5_im.shape[2:], F32)
    cc0 = jnp.zeros((n_odd, bp) + state_convC.shape[2:], F32)
    cfg_prompt = dict(conv_a=dict(ns=1, tb=256, rows=16, shift=True), conv_c=dict(ns=1, tb=256, rows=16),
                      s5=dict(ns=bp, tb=128))
    t_s = x_sample.shape[1]
    cfg_sample = dict(conv_a=dict(ns=16, tb=t_s, rows=t_s, shift=False), conv_c=dict(ns=16, tb=t_s, rows=t_s),
                      s5=dict(ns=32, tb=t_s))
    y_p, ca_p, sr_p, si_p, cc_p = _trunk(x_prompt, p_prompt, ca0, sr0, si0, cc0, w, cfg_prompt)
    y_s, ca_s, sr_s, si_s, cc_s = _trunk(x_sample, p_sample, state_convA, state_s5_re, state_s5_im, state_convC, w,
                                         cfg_sample)
    return (y_p, y_s, ca_p, ca_s, sr_p, si_p, sr_s, si_s, cc_p, cc_s)
```

```python
import functools

import jax
import jax.numpy as jnp
from jax import lax
from jax.experimental import pallas as pl
from jax.experimental.pallas import tpu as pltpu

F32 = jnp.float32
BF16 = jnp.bfloat16
EPS = 1e-6

LANES = 128
SUBLANES = 8
VMEM_LIMIT = 56 * 1024 * 1024

S5_GROUP = 16
S5_STATE = 64
S5_TILE_GROUPS = 16
SCAN_SHIFTS = (1, 2, 4)
S5_SCAN_SEQS = 4
S5_SCAN_LANES = 2 * LANES
CONV_A_WIDTH = 31
CONV_C_WIDTH = 3
CONV_A_PAD = 32
CONV_C_PAD = 8
FFN_DOWN_COLS = 512
CONV_SHIFT_ROWS = 40


def _dot(a, b):
    return jnp.dot(a, b, preferred_element_type=F32)


def _rmsnorm_bf16(x, g):
    ms = jnp.mean(x * x, axis=-1, keepdims=True)
    return (x * lax.rsqrt(ms + EPS) * g).astype(BF16)


def _params(*sem):
    return pltpu.CompilerParams(dimension_semantics=sem, vmem_limit_bytes=VMEM_LIMIT)


def _resident(shape):
    nd = len(shape)
    return pl.BlockSpec(shape, lambda *_: (0,) * nd, pipeline_mode=pl.Buffered(1))


def _ffn_kernel(x_ref, g_ref, wg_ref, wu_ref, wd_ref, o_ref, xn_ref):
    @pl.when(pl.program_id(1) == 0)
    def _():
        x = x_ref[...]
        xn_ref[...] = _rmsnorm_bf16(x, g_ref[...])
        o_ref[...] = x

    xn = xn_ref[...]
    gate = _dot(xn, wg_ref[...].astype(BF16))
    up = _dot(xn, wu_ref[...].astype(BF16))
    act = (gate * jax.nn.sigmoid(gate) * (0.5 * up)).astype(BF16)
    wd = wd_ref[...].astype(BF16)
    for c in range(o_ref.shape[1] // FFN_DOWN_COLS):
        cols = slice(c * FFN_DOWN_COLS, (c + 1) * FFN_DOWN_COLS)
        o_ref[:, cols] += _dot(act, wd[:, cols])


def _ffn(x, g, wg, wu, wd, layer, *, tm=1024, tf=256):
    m, d = x.shape
    f = wg.shape[-1]
    return pl.pallas_call(
        _ffn_kernel,
        grid=(m // tm, f // tf),
        in_specs=[
            pl.BlockSpec((tm, d), lambda i, j: (i, 0)),
            pl.BlockSpec((1, d), lambda i, j: (0, 0)),
            pl.BlockSpec((None, d, tf), lambda i, j: (layer, 0, j)),
            pl.BlockSpec((None, d, tf), lambda i, j: (layer, 0, j)),
            pl.BlockSpec((None, tf, d), lambda i, j: (layer, j, 0)),
        ],
        out_specs=pl.BlockSpec((tm, d), lambda i, j: (i, 0)),
        out_shape=jax.ShapeDtypeStruct((m, d), F32),
        scratch_shapes=[pltpu.VMEM((tm, d), BF16)],
        compiler_params=_params("parallel", "arbitrary"),
        name="ffn",
    )(x, g, wg, wu, wd)


def _inab_kernel(x_ref, g_ref, wv_ref, wgt_ref, wu_ref, a_ref, u_ref, xn_ref):
    @pl.when(pl.program_id(1) == 0)
    def _():
        xn_ref[...] = _rmsnorm_bf16(x_ref[...], g_ref[...])

    xn = xn_ref[...]
    a_ref[...] = _dot(xn, wv_ref[...]) * jax.nn.sigmoid(_dot(xn, wgt_ref[...]))
    u_ref[...] = _dot(xn, wu_ref[...])


def _in_ab(x, g, w, *, tm=1024, tn=512):
    m, d = x.shape
    c = w.shape[1] // 3
    nb = c // tn
    wspec = lambda k: pl.BlockSpec((d, tn), lambda i, j: (0, j + k * nb))
    ospec = pl.BlockSpec((tm, tn), lambda i, j: (i, j))
    return pl.pallas_call(
        _inab_kernel,
        grid=(m // tm, nb),
        in_specs=[pl.BlockSpec((tm, d), lambda i, j: (i, 0)), pl.BlockSpec((1, d), lambda i, j: (0, 0)),
                  wspec(0), wspec(1), wspec(2)],
        out_specs=[ospec, ospec],
        out_shape=[jax.ShapeDtypeStruct((m, c), F32)] * 2,
        scratch_shapes=[pltpu.VMEM((tm, d), BF16)],
        compiler_params=_params("parallel", "arbitrary"),
        name="in_ab",
    )(x, g, w, w, w)


def _inc_kernel(x_ref, g_ref, wb_ref, wc_ref, wv_ref, bg_ref, cv_ref, xn_ref):
    @pl.when(pl.program_id(1) == 0)
    def _():
        xn_ref[...] = _rmsnorm_bf16(x_ref[...], g_ref[...])

    xn = xn_ref[...]
    bg_ref[...] = _dot(xn, wb_ref[...])
    cv_ref[...] = _dot(xn, wc_ref[...]) * _dot(xn, wv_ref[...])


def _in_c(x, g, w, *, tm=1024, tn=512):
    m, d = x.shape
    c = w.shape[1] // 3
    nb = c // tn
    wspec = lambda k: pl.BlockSpec((d, tn), lambda i, j: (0, j + k * nb))
    ospec = pl.BlockSpec((tm, tn), lambda i, j: (i, j))
    return pl.pallas_call(
        _inc_kernel,
        grid=(m // tm, nb),
        in_specs=[pl.BlockSpec((tm, d), lambda i, j: (i, 0)), pl.BlockSpec((1, d), lambda i, j: (0, 0)),
                  wspec(0), wspec(1), wspec(2)],
        out_specs=[ospec, ospec],
        out_shape=[jax.ShapeDtypeStruct((m, c), F32)] * 2,
        scratch_shapes=[pltpu.VMEM((tm, d), BF16)],
        compiler_params=_params("parallel", "arbitrary"),
        name="in_c",
    )(x, g, w, w, w)


def _outproj_kernel(*refs, ksizes, tn):
    n = len(ksizes)
    act_refs, (w_ref, h_ref, o_ref) = refs[:n], refs[n:]
    acts = [a[...].astype(BF16) for a in act_refs]
    for c in range(o_ref.shape[1] // tn):
        cols = slice(c * tn, (c + 1) * tn)
        acc = h_ref[:, cols]
        off = 0
        for a, k in zip(acts, ksizes):
            acc = acc + _dot(a, w_ref[off:off + k, cols])
            off += k
        o_ref[:, cols] = acc


def _out_proj(acts, w, h, *, tm=512, tn=512):
    m, d = h.shape
    ksizes = tuple(a.shape[1] for a in acts)
    row = lambda k: pl.BlockSpec((tm, k), lambda i: (i, 0))
    return pl.pallas_call(
        functools.partial(_outproj_kernel, ksizes=ksizes, tn=tn),
        grid=(m // tm,),
        in_specs=[row(k) for k in ksizes] + [_resident(w.shape), row(d)],
        out_specs=row(d),
        out_shape=jax.ShapeDtypeStruct((m, d), F32),
        compiler_params=_params("parallel"),
        name="out_proj",
    )(*acts, w, h)


def _ple_kernel(*refs, final, tn):
    if final:
        x_ref, p_ref, g_ref, wg_ref, wp_ref, gf_ref, o_ref = refs
    else:
        x_ref, p_ref, g_ref, wg_ref, wp_ref, o_ref = refs
    d = o_ref.shape[1]
    xn = _rmsnorm_bf16(x_ref[...], g_ref[...])
    pb = p_ref[...].astype(BF16)
    ssq = jnp.zeros((o_ref.shape[0], 1), F32)
    for c in range(d // tn):
        cols = slice(c * tn, (c + 1) * tn)
        gate = jax.nn.sigmoid(_dot(xn, wg_ref[:, cols]))
        y = x_ref[:, cols] + gate * _dot(pb, wp_ref[:, cols])
        ssq = ssq + jnp.sum(y * y, axis=-1, keepdims=True)
        o_ref[:, cols] = y
    if final:
        o_ref[...] = o_ref[...] * lax.rsqrt(ssq / d + EPS) * gf_ref[...]


def _ple(x, p, layer, g, wg, wp, gf, *, tm=512, tn=512):
    m, d = x.shape
    dp = p.shape[-1]
    final = gf is not None
    in_specs = [
        pl.BlockSpec((tm, d), lambda i: (i, 0)),
        pl.BlockSpec((None, tm, dp), lambda i: (layer, i, 0)),
        _resident((1, d)),
        _resident(wg.shape),
        _resident(wp.shape),
    ]
    args = [x, p, g, wg, wp]
    if final:
        in_specs.append(_resident((1, d)))
        args.append(gf)
    return pl.pallas_call(
        functools.partial(_ple_kernel, final=final, tn=tn),
        grid=(m // tm,),
        in_specs=in_specs,
        out_specs=pl.BlockSpec((tm, d), lambda i: (i, 0)),
        out_shape=jax.ShapeDtypeStruct((m, d), F32),
        compiler_params=_params("parallel"),
        name="ple",
    )(*args)


def _conva_kernel(x_ref, prev_ref, w_ref, b_ref, g_ref, beta_ref, o_ref, st_ref, buf_ref, *shift_refs, ns, tb, nt, rows):
    t = pl.program_id(1)
    pad, k_w = CONV_A_PAD, CONV_A_WIDTH
    first = pad - (k_w - 1)

    @pl.when(t == 0)
    def _():
        buf_ref[:, 0:pad, :] = prev_ref[...]

    buf_ref[:, pad:pad + tb, :] = x_ref[...]

    def tap(s, r, k):
        off = first + k
        if not shift_refs:
            return buf_ref[s, pl.ds(r * rows + off, rows), :]
        j = off % SUBLANES
        if j == 0:
            return buf_ref[s, pl.ds(r * rows + off, rows), :]
        return shift_refs[0][j - 1, pl.ds(r * rows + off - j, rows), :]

    def per_seq(s, carry):
        if shift_refs:
            span = shift_refs[0].shape[1]
            for j in range(1, SUBLANES):
                for r0 in range(0, span, CONV_SHIFT_ROWS):
                    shift_refs[0][j - 1, r0:r0 + CONV_SHIFT_ROWS, :] = buf_ref[s, pl.ds(r0 + j, CONV_SHIFT_ROWS), :]
        for r in range(tb // rows):
            acc = None
            for k in range(k_w):
                term = tap(s, r, k) * w_ref[k:k + 1, :]
                acc = term if acc is None else acc + term
            y = acc + b_ref[...]
            mu = jnp.mean(y, axis=-1, keepdims=True)
            yc = y - mu
            var = jnp.mean(yc * yc, axis=-1, keepdims=True)
            z = yc * lax.rsqrt(var + EPS) * g_ref[...] + beta_ref[...]
            o_ref[s, r * rows:(r + 1) * rows, :] = z * jax.nn.sigmoid(z)
        return carry

    lax.fori_loop(0, ns, per_seq, 0)

    @pl.when(t == nt - 1)
    def _():
        st_ref[...] = buf_ref[:, tb + first:tb + pad, :]

    if nt > 1:
        buf_ref[:, 0:pad, :] = buf_ref[:, tb:tb + pad, :]


def _conv_a(x, prev, w, b, g, beta, *, ns, tb, rows, shift):
    nseq, t_len, c = x.shape
    nt = t_len // tb
    prev = jnp.pad(prev, ((0, 0), (CONV_A_PAD - (CONV_A_WIDTH - 1), 0), (0, 0)))
    scratch = [pltpu.VMEM((ns, CONV_A_PAD + tb, c), F32)]
    if shift:
        span = tb + CONV_A_PAD - SUBLANES
        assert span % CONV_SHIFT_ROWS == 0
        scratch.append(pltpu.VMEM((SUBLANES - 1, span, c), F32))
    return pl.pallas_call(
        functools.partial(_conva_kernel, ns=ns, tb=tb, nt=nt, rows=rows),
        grid=(nseq // ns, nt),
        in_specs=[
            pl.BlockSpec((ns, tb, c), lambda i, j: (i, j, 0)),
            pl.BlockSpec((ns, CONV_A_PAD, c), lambda i, j: (i, 0, 0)),
            pl.BlockSpec((CONV_A_WIDTH, c), lambda i, j: (0, 0)),
            pl.BlockSpec((1, c), lambda i, j: (0, 0)),
            pl.BlockSpec((1, c), lambda i, j: (0, 0)),
            pl.BlockSpec((1, c), lambda i, j: (0, 0)),
        ],
        out_specs=[
            pl.BlockSpec((ns, tb, c), lambda i, j: (i, j, 0)),
            pl.BlockSpec((ns, CONV_A_WIDTH - 1, c), lambda i, j: (i, 0, 0)),
        ],
        out_shape=[jax.ShapeDtypeStruct((nseq, t_len, c), F32),
                   jax.ShapeDtypeStruct((nseq, CONV_A_WIDTH - 1, c), F32)],
        scratch_shapes=scratch,
        compiler_params=_params("parallel", "arbitrary"),
        name="conv_a",
    )(x, prev, w, b, g, beta)


def _convc_kernel(bg_ref, cv_ref, prev_ref, w_ref, o_ref, st_ref, buf_ref, *, ns, tb, nt, rows):
    t = pl.program_id(1)
    pad, k_w = CONV_C_PAD, CONV_C_WIDTH
    first = pad - (k_w - 1)

    @pl.when(t == 0)
    def _():
        buf_ref[:, 0:pad, :] = prev_ref[...]

    buf_ref[:, pad:pad + tb, :] = cv_ref[...]

    def per_seq(s, carry):
        for r in range(tb // rows):
            acc = None
            for k in range(k_w):
                term = buf_ref[s, pl.ds(r * rows + first + k, rows), :] * w_ref[k:k + 1, :]
                acc = term if acc is None else acc + term
            o_ref[s, r * rows:(r + 1) * rows, :] = bg_ref[s, r * rows:(r + 1) * rows, :] * acc
        return carry

    lax.fori_loop(0, ns, per_seq, 0)

    @pl.when(t == nt - 1)
    def _():
        st_ref[...] = buf_ref[:, tb + first:tb + pad, :]

    if nt > 1:
        buf_ref[:, 0:pad, :] = buf_ref[:, tb:tb + pad, :]


def _conv_c(bg, cv, prev, w, *, ns, tb, rows):
    nseq, t_len, c = cv.shape
    nt = t_len // tb
    prev = jnp.pad(prev, ((0, 0), (CONV_C_PAD - (CONV_C_WIDTH - 1), 0), (0, 0)))
    blk = pl.BlockSpec((ns, tb, c), lambda i, j: (i, j, 0))
    return pl.pallas_call(
        functools.partial(_convc_kernel, ns=ns, tb=tb, nt=nt, rows=rows),
        grid=(nseq // ns, nt),
        in_specs=[blk, blk,
                  pl.BlockSpec((ns, CONV_C_PAD, c), lambda i, j: (i, 0, 0)),
                  pl.BlockSpec((CONV_C_WIDTH, c), lambda i, j: (0, 0))],
        out_specs=[blk, pl.BlockSpec((ns, CONV_C_WIDTH - 1, c), lambda i, j: (i, 0, 0))],
        out_shape=[jax.ShapeDtypeStruct((nseq, t_len, c), F32),
                   jax.ShapeDtypeStruct((nseq, CONV_C_WIDTH - 1, c), F32)],
        scratch_shapes=[pltpu.VMEM((ns, CONV_C_PAD + tb, c), F32)],
        compiler_params=_params("parallel", "arbitrary"),
        name="conv_c",
    )(bg, cv, prev, w)


def _s5prep_kernel(lr_ref, li_ref, ldt_ref, br_ref, bi_ref, pwr_ref, pwi_ref, bbr_ref, bbi_ref):
    lr, li = lr_ref[...], li_ref[...]
    dt = jnp.exp(ldt_ref[...])
    mag = jnp.exp(lr * dt)
    ar, ai = mag * jnp.cos(li * dt), mag * jnp.sin(li * dt)
    den = lr * lr + li * li
    qr = ((ar - 1.0) * lr + ai * li) / den
    qi = (ai * lr - (ar - 1.0) * li) / den
    br, bi = br_ref[...], bi_ref[...]
    bbr_ref[...] = qr * br - qi * bi
    bbi_ref[...] = qr * bi + qi * br
    pr, pi = ar, ai
    for r in range(SUBLANES):
        pwr_ref[:, r:r + 1, :] = pr
        pwi_ref[:, r:r + 1, :] = pi
        pr, pi = pr * ar - pi * ai, pr * ai + pi * ar


def _s5_prep(lam_re, lam_im, log_dt, b_re, b_im):
    g, n = lam_re.shape
    h = b_re.shape[-1]
    lam3 = lambda a: a.reshape(g, 1, n)
    ldt = jnp.broadcast_to(log_dt.reshape(g, 1, 1), (g, 1, n))
    bt = lambda a: jnp.transpose(a, (0, 2, 1))
    small = jax.ShapeDtypeStruct((g, SUBLANES, n), F32)
    big = jax.ShapeDtypeStruct((g, h, n), F32)
    return pl.pallas_call(_s5prep_kernel, out_shape=[small, small, big, big], name="s5_prep")(
        lam3(lam_re), lam3(lam_im), ldt, bt(b_re), bt(b_im))


def _scan_tiles(pw):
    g, r, n = pw.shape
    carry = jnp.transpose(pw, (1, 0, 2)).reshape(r, g * n)
    row = jnp.arange(r)[:, None]
    levels = [jnp.where(row >= sh, carry[sh - 1:sh, :], 0.0) for sh in SCAN_SHIFTS]
    return carry, jnp.stack(levels)


def _block_diag_tiles(x):
    g, r, c = x.shape
    tg = S5_TILE_GROUPS
    eye = jnp.eye(tg, dtype=x.dtype)
    y = x.reshape(g // tg, tg, r, 1, c) * eye[None, :, None, :, None]
    return y.reshape(g // tg, tg * r, tg * c)


def _s5_kernel(u_ref, h0r_ref, h0i_ref, pcr_ref, pci_ref, plr_ref, pli_ref, bre_ref, bim_ref, cre_ref, cim_ref, d_ref,
               wglu_ref, bglu_ref, y_ref, hr_ref, hi_ref, bur_ref, bui_ref, *, ns, tb):
    n_tiles, k_in, k_st = bre_ref.shape
    rows = ns * tb
    n_state = n_tiles * k_st
    width = S5_SCAN_LANES
    t_tiles = tb // SUBLANES

    @pl.when(pl.program_id(1) == 0)
    def _():
        hr_ref[...] = h0r_ref[...]
        hi_ref[...] = h0i_ref[...]

    u = u_ref[...].reshape(rows, n_tiles * k_in)
    ub = u.astype(BF16)
    for k in range(n_tiles):
        uk = ub[:, k * k_in:(k + 1) * k_in]
        bur_ref[:, k * k_st:(k + 1) * k_st] = _dot(uk, bre_ref[k])
        bui_ref[:, k * k_st:(k + 1) * k_st] = _dot(uk, bim_ref[k])

    def scan_group(grp):
        for c in range(n_state // width):
            lanes = slice(c * width, (c + 1) * width)

            def tile(i, carry, lanes=lanes):
                out = []
                for s in range(S5_SCAN_SEQS):
                    hr, hi = carry[2 * s], carry[2 * s + 1]
                    at = pl.ds(pl.multiple_of((grp * S5_SCAN_SEQS + s) * tb + i * SUBLANES, SUBLANES), SUBLANES)
                    xr, xi = bur_ref[at, lanes], bui_ref[at, lanes]
                    for k, sh in enumerate(SCAN_SHIFTS):
                        sr, si = pltpu.roll(xr, sh, axis=0), pltpu.roll(xi, sh, axis=0)
                        ar, ai = plr_ref[k, :, lanes], pli_ref[k, :, lanes]
                        xr, xi = xr + (ar * sr - ai * si), xi + (ar * si + ai * sr)
                    cr, ci = jnp.broadcast_to(hr, xr.shape), jnp.broadcast_to(hi, xi.shape)
                    pr, pi = pcr_ref[:, lanes], pci_ref[:, lanes]
                    xr, xi = xr + (pr * cr - pi * ci), xi + (pr * ci + pi * cr)
                    bur_ref[at, lanes] = xr
                    bui_ref[at, lanes] = xi
                    out += [xr[SUBLANES - 1:, :], xi[SUBLANES - 1:, :]]
                return tuple(out)

            init = []
            for s in range(S5_SCAN_SEQS):
                seq = pl.ds(grp * S5_SCAN_SEQS + s, 1)
                init += [hr_ref[seq, lanes], hi_ref[seq, lanes]]
            last = tile(0, tuple(init)) if t_tiles == 1 else lax.fori_loop(0, t_tiles, tile, tuple(init))
            for s in range(S5_SCAN_SEQS):
                seq = pl.ds(grp * S5_SCAN_SEQS + s, 1)
                hr_ref[seq, lanes] = last[2 * s]
                hi_ref[seq, lanes] = last[2 * s + 1]

    n_groups = ns // S5_SCAN_SEQS
    if n_groups == 1:
        scan_group(0)
    else:
        def group_step(grp, carry):
            scan_group(grp)
            return carry
        lax.fori_loop(0, n_groups, group_step, 0)

    ys = []
    for k in range(n_tiles):
        st = slice(k * k_st, (k + 1) * k_st)
        ys.append(_dot(bur_ref[:, st].astype(BF16), cre_ref[k]) - _dot(bui_ref[:, st].astype(BF16), cim_ref[k]))
    y = jnp.concatenate(ys, axis=-1) + d_ref[...] * u
    y = jax.nn.gelu(y)
    y = y * jax.nn.sigmoid(_dot(y.astype(BF16), wglu_ref[...]) + bglu_ref[...])
    y_ref[...] = y.reshape(y_ref.shape)


def _s5(u, h0r, h0i, pcr, pci, plr, pli, bre, bim, cre, cim, d, wglu, bglu, *, ns, tb):
    nseq, t_len, c = u.shape
    n_state = h0r.shape[1]
    blk = pl.BlockSpec((ns, tb, c), lambda i, j: (i, j, 0))
    st = pl.BlockSpec((ns, n_state), lambda i, j: (i, 0))
    st_shape = jax.ShapeDtypeStruct((nseq, n_state), F32)
    consts = (pcr, pci, plr, pli, bre, bim, cre, cim, d, wglu, bglu)
    return pl.pallas_call(
        functools.partial(_s5_kernel, ns=ns, tb=tb),
        grid=(nseq // ns, t_len // tb),
        in_specs=[blk, st, st] + [_resident(a.shape) for a in consts],
        out_specs=[blk, st, st],
        out_shape=[jax.ShapeDtypeStruct((nseq, t_len, c), F32), st_shape, st_shape],
        scratch_shapes=[pltpu.VMEM((ns * tb, n_state), F32), pltpu.VMEM((ns * tb, n_state), F32)],
        compiler_params=_params("parallel", "arbitrary"),
        name="s5",
    )(u, h0r, h0i, *consts)


def _trunk(x, p, conv_a, s5_re, s5_im, conv_c, w, cfg):
    nseq, t_len, d = x.shape
    m = nseq * t_len
    depth = p.shape[0]
    flat = lambda a: a.reshape(m, a.shape[-1])
    seq = lambda a: a.reshape(nseq, t_len, a.shape[-1])
    x = flat(x)
    p = p.reshape(depth, m, p.shape[-1])
    new_ca, new_sr, new_si, new_cc = [], [], [], []
    for i in range(depth):
        j = i // 2
        h = _ffn(x, w["norm_ffn1"][i], w["w_ffn1_gate"], w["w_ffn1_up"], w["w_ffn1_down"], i)
        if i % 2 == 0:
            a, u = _in_ab(h, w["norm_mix"][i], w["w_in_ab"][j])
            a, ca = _conv_a(seq(a), conv_a[j], w["w_dw_a"][j], w["b_dw_a"][j], w["ln_a_g"][j], w["ln_a_b"][j],
                            **cfg["conv_a"])
            s5w = w["s5"][j]
            n_state = s5_re.shape[-2] * s5_re.shape[-1]
            yb, sr, si = _s5(seq(u), s5_re[j].reshape(nseq, n_state), s5_im[j].reshape(nseq, n_state), *s5w,
                             w["w_glu_b"][j], w["b_glu_b"][j], **cfg["s5"])
            h = _out_proj([flat(a), flat(yb)], w["w_out_ab"][j], h)
            new_ca.append(ca)
            new_sr.append(sr.reshape(s5_re.shape[1:]))
            new_si.append(si.reshape(s5_im.shape[1:]))
        else:
            bg, cv = _in_c(h, w["norm_mix"][i], w["w_in_c"][j])
            mix, cc = _conv_c(seq(bg), seq(cv), conv_c[j], w["w_dw_c"][j], **cfg["conv_c"])
            h = _out_proj([flat(mix)], w["w_out_c"][j], h)
            new_cc.append(cc)
        h = _ffn(h, w["norm_ffn2"][i], w["w_ffn2_gate"], w["w_ffn2_up"], w["w_ffn2_down"], i)
        x = _ple(h, p, i, w["norm_ple"][i], w["w_ple_gate"][i], w["w_ple_proj"][i],
                 w["norm_final"] if i == depth - 1 else None)
    return (x.reshape(nseq, t_len, d), jnp.stack(new_ca), jnp.stack(new_sr), jnp.stack(new_si),
            jnp.stack(new_cc))


def _prepare_weights(norm_ffn1, w_ffn1_gate, w_ffn1_up, w_ffn1_down, norm_mix, norm_ffn2, w_ffn2_gate, w_ffn2_up,
                     w_ffn2_down, norm_ple, w_ple_gate, w_ple_proj, w_in_ab, w_dw_a, b_dw_a, ln_a_g, ln_a_b,
                     s5_lam_re, s5_lam_im, s5_log_dt, s5_b_re, s5_b_im, s5_c_re, s5_c_im, s5_d, w_glu_b, b_glu_b,
                     w_out_ab, w_in_c, w_dw_c, w_out_c, norm_final):
    row = lambda a: a[..., None, :]
    bf = lambda a: a.astype(BF16)
    w = dict(
        norm_ffn1=row(norm_ffn1), norm_mix=row(norm_mix), norm_ffn2=row(norm_ffn2), norm_ple=row(norm_ple),
        norm_final=norm_final[None, :],
        w_ffn1_gate=w_ffn1_gate, w_ffn1_up=w_ffn1_up, w_ffn1_down=w_ffn1_down,
        w_ffn2_gate=w_ffn2_gate, w_ffn2_up=w_ffn2_up, w_ffn2_down=w_ffn2_down,
        w_ple_gate=bf(w_ple_gate), w_ple_proj=bf(w_ple_proj),
        w_in_ab=bf(w_in_ab), w_out_ab=bf(w_out_ab), w_glu_b=bf(w_glu_b), b_glu_b=row(b_glu_b),
        w_dw_a=w_dw_a, b_dw_a=row(b_dw_a), ln_a_g=row(ln_a_g), ln_a_b=row(ln_a_b),
        w_in_c=bf(w_in_c), w_dw_c=w_dw_c, w_out_c=bf(w_out_c),
    )
    s5 = []
    for j in range(s5_lam_re.shape[0]):
        pwr, pwi, bbr, bbi = _s5_prep(s5_lam_re[j], s5_lam_im[j], s5_log_dt[j], s5_b_re[j], s5_b_im[j])
        ct = lambda a: jnp.transpose(a, (0, 2, 1))
        pcr, plr = _scan_tiles(pwr)
        pci, pli = _scan_tiles(pwi)
        s5.append((pcr, pci, plr, pli,
                   bf(_block_diag_tiles(bbr)), bf(_block_diag_tiles(bbi)),
                   bf(_block_diag_tiles(ct(s5_c_re[j]))), bf(_block_diag_tiles(ct(s5_c_im[j]))),
                   s5_d[j][None, :]))
    w["s5"] = s5
    return w


def kernel(x_prompt, x_sample, p_prompt, p_sample, state_convA, state_s5_re, state_s5_im, state_convC, norm_ffn1, w_ffn1_gate, w_ffn1_up, w_ffn1_down, norm_mix, norm_ffn2, w_ffn2_gate, w_ffn2_up, w_ffn2_down, norm_ple, w_ple_gate, w_ple_proj, w_in_ab, w_dw_a, b_dw_a, ln_a_g, ln_a_b, s5_lam_re, s5_lam_im, s5_log_dt, s5_b_re, s5_b_im, s5_c_re, s5_c_im, s5_d, w_glu_b, b_glu_b, w_out_ab, w_in_c, w_dw_c, w_out_c, norm_final):
    w = _prepare_weights(norm_ffn1, w_ffn1_gate, w_ffn1_up, w_ffn1_down, norm_mix, norm_ffn2, w_ffn2_gate,
                         w_ffn2_up, w_ffn2_down, norm_ple, w_ple_gate, w_ple_proj, w_in_ab, w_dw_a, b_dw_a,
                         ln_a_g, ln_a_b, s5_lam_re, s5_lam_im, s5_log_dt, s5_b_re, s5_b_im, s5_c_re, s5_c_im,
                         s5_d, w_glu_b, b_glu_b, w_out_ab, w_in_c, w_dw_c, w_out_c, norm_final)
    bp = x_prompt.shape[0]
    n_even, n_odd = state_convA.shape[0], state_convC.shape[0]
    ca0 = jnp.zeros((n_even, bp) + state_convA.shape[2:], F32)
    sr0 = jnp.zeros((n_even, bp) + state_s5_re.shape[2:], F32)
    si0 = jnp.zeros((n_even, bp) + state_s5_im.shape[2:], F32)
    cc0 = jnp.zeros((n_odd, bp) + state_convC.shape[2:], F32)
    cfg_prompt = dict(conv_a=dict(ns=1, tb=256, rows=16, shift=True), conv_c=dict(ns=1, tb=256, rows=16),
                      s5=dict(ns=bp, tb=128))
    t_s = x_sample.shape[1]
    cfg_sample = dict(conv_a=dict(ns=16, tb=t_s, rows=t_s, shift=False), conv_c=dict(ns=16, tb=t_s, rows=t_s),
                      s5=dict(ns=32, tb=t_s))
    y_p, ca_p, sr_p, si_p, cc_p = _trunk(x_prompt, p_prompt, ca0, sr0, si0, cc0, w, cfg_prompt)
    y_s, ca_s, sr_s, si_s, cc_s = _trunk(x_sample, p_sample, state_convA, state_s5_re, state_s5_im, state_convC, w,
                                         cfg_sample)
    return (y_p, y_s, ca_p, ca_s, sr_p, si_p, sr_s, si_s, cc_p, cc_s)
```

```python
import functools

import jax
import jax.numpy as jnp
from jax import lax
from jax.experimental import pallas as pl
from jax.experimental.pallas import tpu as pltpu

F32 = jnp.float32
BF16 = jnp.bfloat16
EPS = 1e-6

LANES = 128
SUBLANES = 8
VMEM_LIMIT = 56 * 1024 * 1024

S5_GROUP = 16
S5_STATE = 64
S5_TILE_GROUPS = 16
SCAN_SHIFTS = (1, 2, 4)
S5_SCAN_SEQS = 4
S5_SCAN_LANES = 4 * LANES
CONV_A_WIDTH = 31
CONV_C_WIDTH = 3
CONV_A_PAD = 32
CONV_C_PAD = 8
FFN_DOWN_COLS = 512
CONV_SHIFT_ROWS = 40


def _dot(a, b):
    return jnp.dot(a, b, preferred_element_type=F32)


def _rmsnorm_bf16(x, g):
    ms = jnp.mean(x * x, axis=-1, keepdims=True)
    return (x * lax.rsqrt(ms + EPS) * g).astype(BF16)


def _params(*sem):
    return pltpu.CompilerParams(dimension_semantics=sem, vmem_limit_bytes=VMEM_LIMIT)


def _resident(shape):
    nd = len(shape)
    return pl.BlockSpec(shape, lambda *_: (0,) * nd, pipeline_mode=pl.Buffered(1))


def _ffn_kernel(x_ref, g_ref, wg_ref, wu_ref, wd_ref, o_ref, xn_ref):
    @pl.when(pl.program_id(1) == 0)
    def _():
        x = x_ref[...]
        xn_ref[...] = _rmsnorm_bf16(x, g_ref[...])
        o_ref[...] = x

    xn = xn_ref[...]
    gate = _dot(xn, wg_ref[...].astype(BF16))
    up = _dot(xn, wu_ref[...].astype(BF16))
    act = (gate * jax.nn.sigmoid(gate) * (0.5 * up)).astype(BF16)
    wd = wd_ref[...].astype(BF16)
    for c in range(o_ref.shape[1] // FFN_DOWN_COLS):
        cols = slice(c * FFN_DOWN_COLS, (c + 1) * FFN_DOWN_COLS)
        o_ref[:, cols] += _dot(act, wd[:, cols])


def _ffn(x, g, wg, wu, wd, layer, *, tm=1024, tf=256):
    m, d = x.shape
    f = wg.shape[-1]
    return pl.pallas_call(
        _ffn_kernel,
        grid=(m // tm, f // tf),
        in_specs=[
            pl.BlockSpec((tm, d), lambda i, j: (i, 0)),
            pl.BlockSpec((1, d), lambda i, j: (0, 0)),
            pl.BlockSpec((None, d, tf), lambda i, j: (layer, 0, j)),
            pl.BlockSpec((None, d, tf), lambda i, j: (layer, 0, j)),
            pl.BlockSpec((None, tf, d), lambda i, j: (layer, j, 0)),
        ],
        out_specs=pl.BlockSpec((tm, d), lambda i, j: (i, 0)),
        out_shape=jax.ShapeDtypeStruct((m, d), F32),
        scratch_shapes=[pltpu.VMEM((tm, d), BF16)],
        compiler_params=_params("parallel", "arbitrary"),
        name="ffn",
    )(x, g, wg, wu, wd)


def _inab_kernel(x_ref, g_ref, wv_ref, wgt_ref, wu_ref, a_ref, u_ref, xn_ref):
    @pl.when(pl.program_id(1) == 0)
    def _():
        xn_ref[...] = _rmsnorm_bf16(x_ref[...], g_ref[...])

    xn = xn_ref[...]
    a_ref[...] = _dot(xn, wv_ref[...]) * jax.nn.sigmoid(_dot(xn, wgt_ref[...]))
    u_ref[...] = _dot(xn, wu_ref[...])


def _in_ab(x, g, w, *, tm=1024, tn=512):
    m, d = x.shape
    c = w.shape[1] // 3
    nb = c // tn
    wspec = lambda k: pl.BlockSpec((d, tn), lambda i, j: (0, j + k * nb))
    ospec = pl.BlockSpec((tm, tn), lambda i, j: (i, j))
    return pl.pallas_call(
        _inab_kernel,
        grid=(m // tm, nb),
        in_specs=[pl.BlockSpec((tm, d), lambda i, j: (i, 0)), pl.BlockSpec((1, d), lambda i, j: (0, 0)),
                  wspec(0), wspec(1), wspec(2)],
        out_specs=[ospec, ospec],
        out_shape=[jax.ShapeDtypeStruct((m, c), F32)] * 2,
        scratch_shapes=[pltpu.VMEM((tm, d), BF16)],
        compiler_params=_params("parallel", "arbitrary"),
        name="in_ab",
    )(x, g, w, w, w)


def _inc_kernel(x_ref, g_ref, wb_ref, wc_ref, wv_ref, bg_ref, cv_ref, xn_ref):
    @pl.when(pl.program_id(1) == 0)
    def _():
        xn_ref[...] = _rmsnorm_bf16(x_ref[...], g_ref[...])

    xn = xn_ref[...]
    bg_ref[...] = _dot(xn, wb_ref[...])
    cv_ref[...] = _dot(xn, wc_ref[...]) * _dot(xn, wv_ref[...])


def _in_c(x, g, w, *, tm=1024, tn=512):
    m, d = x.shape
    c = w.shape[1] // 3
    nb = c // tn
    wspec = lambda k: pl.BlockSpec((d, tn), lambda i, j: (0, j + k * nb))
    ospec = pl.BlockSpec((tm, tn), lambda i, j: (i, j))
    return pl.pallas_call(
        _inc_kernel,
        grid=(m // tm, nb),
        in_specs=[pl.BlockSpec((tm, d), lambda i, j: (i, 0)), pl.BlockSpec((1, d), lambda i, j: (0, 0)),
                  wspec(0), wspec(1), wspec(2)],
        out_specs=[ospec, ospec],
        out_shape=[jax.ShapeDtypeStruct((m, c), F32)] * 2,
        scratch_shapes=[pltpu.VMEM((tm, d), BF16)],
        compiler_params=_params("parallel", "arbitrary"),
        name="in_c",
    )(x, g, w, w, w)


def _outproj_kernel(*refs, ksizes, tn):
    n = len(ksizes)
    act_refs, (w_ref, h_ref, o_ref) = refs[:n], refs[n:]
    acts = [a[...].astype(BF16) for a in act_refs]
    for c in range(o_ref.shape[1] // tn):
        cols = slice(c * tn, (c + 1) * tn)
        acc = h_ref[:, cols]
        off = 0
        for a, k in zip(acts, ksizes):
            acc = acc + _dot(a, w_ref[off:off + k, cols])
            off += k
        o_ref[:, cols] = acc


def _out_proj(acts, w, h, *, tm=512, tn=512):
    m, d = h.shape
    ksizes = tuple(a.shape[1] for a in acts)
    row = lambda k: pl.BlockSpec((tm, k), lambda i: (i, 0))
    return pl.pallas_call(
        functools.partial(_outproj_kernel, ksizes=ksizes, tn=tn),
        grid=(m // tm,),
        in_specs=[row(k) for k in ksizes] + [_resident(w.shape), row(d)],
        out_specs=row(d),
        out_shape=jax.ShapeDtypeStruct((m, d), F32),
        compiler_params=_params("parallel"),
        name="out_proj",
    )(*acts, w, h)


def _ple_kernel(*refs, final, tn):
    if final:
        x_ref, p_ref, g_ref, wg_ref, wp_ref, gf_ref, o_ref = refs
    else:
        x_ref, p_ref, g_ref, wg_ref, wp_ref, o_ref = refs
    d = o_ref.shape[1]
    xn = _rmsnorm_bf16(x_ref[...], g_ref[...])
    pb = p_ref[...].astype(BF16)
    ssq = jnp.zeros((o_ref.shape[0], 1), F32)
    for c in range(d // tn):
        cols = slice(c * tn, (c + 1) * tn)
        gate = jax.nn.sigmoid(_dot(xn, wg_ref[:, cols]))
        y = x_ref[:, cols] + gate * _dot(pb, wp_ref[:, cols])
        ssq = ssq + jnp.sum(y * y, axis=-1, keepdims=True)
        o_ref[:, cols] = y
    if final:
        o_ref[...] = o_ref[...] * lax.rsqrt(ssq / d + EPS) * gf_ref[...]


def _ple(x, p, layer, g, wg, wp, gf, *, tm=512, tn=512):
    m, d = x.shape
    dp = p.shape[-1]
    final = gf is not None
    in_specs = [
        pl.BlockSpec((tm, d), lambda i: (i, 0)),
        pl.BlockSpec((None, tm, dp), lambda i: (layer, i, 0)),
        _resident((1, d)),
        _resident(wg.shape),
        _resident(wp.shape),
    ]
    args = [x, p, g, wg, wp]
    if final:
        in_specs.append(_resident((1, d)))
        args.append(gf)
    return pl.pallas_call(
        functools.partial(_ple_kernel, final=final, tn=tn),
        grid=(m // tm,),
        in_specs=in_specs,
        out_specs=pl.BlockSpec((tm, d), lambda i: (i, 0)),
        out_shape=jax.ShapeDtypeStruct((m, d), F32),
        compiler_params=_params("parallel"),
        name="ple",
    )(*args)


def _conva_kernel(x_ref, prev_ref, w_ref, b_ref, g_ref, beta_ref, o_ref, st_ref, buf_ref, *shift_refs, ns, tb, nt, rows):
    t = pl.program_id(1)
    pad, k_w = CONV_A_PAD, CONV_A_WIDTH
    first = pad - (k_w - 1)

    @pl.when(t == 0)
    def _():
        buf_ref[:, 0:pad, :] = prev_ref[...]

    buf_ref[:, pad:pad + tb, :] = x_ref[...]

    def tap(s, r, k):
        off = first + k
        if not shift_refs:
            return buf_ref[s, pl.ds(r * rows + off, rows), :]
        j = off % SUBLANES
        if j == 0:
            return buf_ref[s, pl.ds(r * rows + off, rows), :]
        return shift_refs[0][j - 1, pl.ds(r * rows + off - j, rows), :]

    def per_seq(s, carry):
        if shift_refs:
            span = shift_refs[0].shape[1]
            for j in range(1, SUBLANES):
                for r0 in range(0, span, CONV_SHIFT_ROWS):
                    shift_refs[0][j - 1, r0:r0 + CONV_SHIFT_ROWS, :] = buf_ref[s, pl.ds(r0 + j, CONV_SHIFT_ROWS), :]
        for r in range(tb // rows):
            acc = None
            for k in range(k_w):
                term = tap(s, r, k) * w_ref[k:k + 1, :]
                acc = term if acc is None else acc + term
            y = acc + b_ref[...]
            mu = jnp.mean(y, axis=-1, keepdims=True)
            yc = y - mu
            var = jnp.mean(yc * yc, axis=-1, keepdims=True)
            z = yc * lax.rsqrt(var + EPS) * g_ref[...] + beta_ref[...]
            o_ref[s, r * rows:(r + 1) * rows, :] = z * jax.nn.sigmoid(z)
        return carry

    lax.fori_loop(0, ns, per_seq, 0)

    @pl.when(t == nt - 1)
    def _():
        st_ref[...] = buf_ref[:, tb + first:tb + pad, :]

    if nt > 1:
        buf_ref[:, 0:pad, :] = buf_ref[:, tb:tb + pad, :]


def _conv_a(x, prev, w, b, g, beta, *, ns, tb, rows, shift):
    nseq, t_len, c = x.shape
    nt = t_len // tb
    prev = jnp.pad(prev, ((0, 0), (CONV_A_PAD - (CONV_A_WIDTH - 1), 0), (0, 0)))
    scratch = [pltpu.VMEM((ns, CONV_A_PAD + tb, c), F32)]
    if shift:
        span = tb + CONV_A_PAD - SUBLANES
        assert span % CONV_SHIFT_ROWS == 0
        scratch.append(pltpu.VMEM((SUBLANES - 1, span, c), F32))
    return pl.pallas_call(
        functools.partial(_conva_kernel, ns=ns, tb=tb, nt=nt, rows=rows),
        grid=(nseq // ns, nt),
        in_specs=[
            pl.BlockSpec((ns, tb, c), lambda i, j: (i, j, 0)),
            pl.BlockSpec((ns, CONV_A_PAD, c), lambda i, j: (i, 0, 0)),
            pl.BlockSpec((CONV_A_WIDTH, c), lambda i, j: (0, 0)),
            pl.BlockSpec((1, c), lambda i, j: (0, 0)),
            pl.BlockSpec((1, c), lambda i, j: (0, 0)),
            pl.BlockSpec((1, c), lambda i, j: (0, 0)),
        ],
        out_specs=[
            pl.BlockSpec((ns, tb, c), lambda i, j: (i, j, 0)),
            pl.BlockSpec((ns, CONV_A_WIDTH - 1, c), lambda i, j: (i, 0, 0)),
        ],
        out_shape=[jax.ShapeDtypeStruct((nseq, t_len, c), F32),
                   jax.ShapeDtypeStruct((nseq, CONV_A_WIDTH - 1, c), F32)],
        scratch_shapes=scratch,
        compiler_params=_params("parallel", "arbitrary"),
        name="conv_a",
    )(x, prev, w, b, g, beta)


def _convc_kernel(bg_ref, cv_ref, prev_ref, w_ref, o_ref, st_ref, buf_ref, *, ns, tb, nt, rows):
    t = pl.program_id(1)
    pad, k_w = CONV_C_PAD, CONV_C_WIDTH
    first = pad - (k_w - 1)

    @pl.when(t == 0)
    def _():
        buf_ref[:, 0:pad, :] = prev_ref[...]

    buf_ref[:, pad:pad + tb, :] = cv_ref[...]

    def per_seq(s, carry):
        for r in range(tb // rows):
            acc = None
            for k in range(k_w):
                term = buf_ref[s, pl.ds(r * rows + first + k, rows), :] * w_ref[k:k + 1, :]
                acc = term if acc is None else acc + term
            o_ref[s, r * rows:(r + 1) * rows, :] = bg_ref[s, r * rows:(r + 1) * rows, :] * acc
        return carry

    lax.fori_loop(0, ns, per_seq, 0)

    @pl.when(t == nt - 1)
    def _():
        st_ref[...] = buf_ref[:, tb + first:tb + pad, :]

    if nt > 1:
        buf_ref[:, 0:pad, :] = buf_ref[:, tb:tb + pad, :]


def _conv_c(bg, cv, prev, w, *, ns, tb, rows):
    nseq, t_len, c = cv.shape
    nt = t_len // tb
    prev = jnp.pad(prev, ((0, 0), (CONV_C_PAD - (CONV_C_WIDTH - 1), 0), (0, 0)))
    blk = pl.BlockSpec((ns, tb, c), lambda i, j: (i, j, 0))
    return pl.pallas_call(
        functools.partial(_convc_kernel, ns=ns, tb=tb, nt=nt, rows=rows),
        grid=(nseq // ns, nt),
        in_specs=[blk, blk,
                  pl.BlockSpec((ns, CONV_C_PAD, c), lambda i, j: (i, 0, 0)),
                  pl.BlockSpec((CONV_C_WIDTH, c), lambda i, j: (0, 0))],
        out_specs=[blk, pl.BlockSpec((ns, CONV_C_WIDTH - 1, c), lambda i, j: (i, 0, 0))],
        out_shape=[jax.ShapeDtypeStruct((nseq, t_len, c), F32),
                   jax.ShapeDtypeStruct((nseq, CONV_C_WIDTH - 1, c), F32)],
        scratch_shapes=[pltpu.VMEM((ns, CONV_C_PAD + tb, c), F32)],
        compiler_params=_params("parallel", "arbitrary"),
        name="conv_c",
    )(bg, cv, prev, w)


def _s5prep_kernel(lr_ref, li_ref, ldt_ref, br_ref, bi_ref, pwr_ref, pwi_ref, bbr_ref, bbi_ref):
    lr, li = lr_ref[...], li_ref[...]
    dt = jnp.exp(ldt_ref[...])
    mag = jnp.exp(lr * dt)
    ar, ai = mag * jnp.cos(li * dt), mag * jnp.sin(li * dt)
    den = lr * lr + li * li
    qr = ((ar - 1.0) * lr + ai * li) / den
    qi = (ai * lr - (ar - 1.0) * li) / den
    br, bi = br_ref[...], bi_ref[...]
    bbr_ref[...] = qr * br - qi * bi
    bbi_ref[...] = qr * bi + qi * br
    pr, pi = ar, ai
    for r in range(SUBLANES):
        pwr_ref[:, r:r + 1, :] = pr
        pwi_ref[:, r:r + 1, :] = pi
        pr, pi = pr * ar - pi * ai, pr * ai + pi * ar


def _s5_prep(lam_re, lam_im, log_dt, b_re, b_im):
    g, n = lam_re.shape
    h = b_re.shape[-1]
    lam3 = lambda a: a.reshape(g, 1, n)
    ldt = jnp.broadcast_to(log_dt.reshape(g, 1, 1), (g, 1, n))
    bt = lambda a: jnp.transpose(a, (0, 2, 1))
    small = jax.ShapeDtypeStruct((g, SUBLANES, n), F32)
    big = jax.ShapeDtypeStruct((g, h, n), F32)
    return pl.pallas_call(_s5prep_kernel, out_shape=[small, small, big, big], name="s5_prep")(
        lam3(lam_re), lam3(lam_im), ldt, bt(b_re), bt(b_im))


def _scan_tiles(pw):
    g, r, n = pw.shape
    carry = jnp.transpose(pw, (1, 0, 2)).reshape(r, g * n)
    row = jnp.arange(r)[:, None]
    levels = [jnp.where(row >= sh, carry[sh - 1:sh, :], 0.0) for sh in SCAN_SHIFTS]
    return carry, jnp.stack(levels)


def _block_diag_tiles(x):
    g, r, c = x.shape
    tg = S5_TILE_GROUPS
    eye = jnp.eye(tg, dtype=x.dtype)
    y = x.reshape(g // tg, tg, r, 1, c) * eye[None, :, None, :, None]
    return y.reshape(g // tg, tg * r, tg * c)


def _s5_kernel(u_ref, h0r_ref, h0i_ref, pcr_ref, pci_ref, plr_ref, pli_ref, bre_ref, bim_ref, cre_ref, cim_ref, d_ref,
               wglu_ref, bglu_ref, y_ref, hr_ref, hi_ref, bur_ref, bui_ref, *, ns, tb):
    n_tiles, k_in, k_st = bre_ref.shape
    rows = ns * tb
    n_state = n_tiles * k_st
    width = S5_SCAN_LANES
    t_tiles = tb // SUBLANES

    @pl.when(pl.program_id(1) == 0)
    def _():
        hr_ref[...] = h0r_ref[...]
        hi_ref[...] = h0i_ref[...]

    u = u_ref[...].reshape(rows, n_tiles * k_in)
    ub = u.astype(BF16)
    for k in range(n_tiles):
        uk = ub[:, k * k_in:(k + 1) * k_in]
        bur_ref[:, k * k_st:(k + 1) * k_st] = _dot(uk, bre_ref[k])
        bui_ref[:, k * k_st:(k + 1) * k_st] = _dot(uk, bim_ref[k])

    def scan_group(grp):
        for c in range(n_state // width):
            lanes = slice(c * width, (c + 1) * width)

            def tile(i, carry, lanes=lanes):
                out = []
                for s in range(S5_SCAN_SEQS):
                    hr, hi = carry[2 * s], carry[2 * s + 1]
                    at = pl.ds(pl.multiple_of((grp * S5_SCAN_SEQS + s) * tb + i * SUBLANES, SUBLANES), SUBLANES)
                    xr, xi = bur_ref[at, lanes], bui_ref[at, lanes]
                    for k, sh in enumerate(SCAN_SHIFTS):
                        sr, si = pltpu.roll(xr, sh, axis=0), pltpu.roll(xi, sh, axis=0)
                        ar, ai = plr_ref[k, :, lanes], pli_ref[k, :, lanes]
                        xr, xi = xr + (ar * sr - ai * si), xi + (ar * si + ai * sr)
                    cr, ci = jnp.broadcast_to(hr, xr.shape), jnp.broadcast_to(hi, xi.shape)
                    pr, pi = pcr_ref[:, lanes], pci_ref[:, lanes]
                    xr, xi = xr + (pr * cr - pi * ci), xi + (pr * ci + pi * cr)
                    bur_ref[at, lanes] = xr
                    bui_ref[at, lanes] = xi
                    out += [xr[SUBLANES - 1:, :], xi[SUBLANES - 1:, :]]
                return tuple(out)

            init = []
            for s in range(S5_SCAN_SEQS):
                seq = pl.ds(grp * S5_SCAN_SEQS + s, 1)
                init += [hr_ref[seq, lanes], hi_ref[seq, lanes]]
            last = tile(0, tuple(init)) if t_tiles == 1 else lax.fori_loop(0, t_tiles, tile, tuple(init))
            for s in range(S5_SCAN_SEQS):
                seq = pl.ds(grp * S5_SCAN_SEQS + s, 1)
                hr_ref[seq, lanes] = last[2 * s]
                hi_ref[seq, lanes] = last[2 * s + 1]

    n_groups = ns // S5_SCAN_SEQS
    if n_groups == 1:
        scan_group(0)
    else:
        def group_step(grp, carry):
            scan_group(grp)
            return carry
        lax.fori_loop(0, n_groups, group_step, 0)

    ys = []
    for k in range(n_tiles):
        st = slice(k * k_st, (k + 1) * k_st)
        ys.append(_dot(bur_ref[:, st].astype(BF16), cre_ref[k]) - _dot(bui_ref[:, st].astype(BF16), cim_ref[k]))
    y = jnp.concatenate(ys, axis=-1) + d_ref[...] * u
    y = jax.nn.gelu(y)
    y = y * jax.nn.sigmoid(_dot(y.astype(BF16), wglu_ref[...]) + bglu_ref[...])
    y_ref[...] = y.reshape(y_ref.shape)


def _s5(u, h0r, h0i, pcr, pci, plr, pli, bre, bim, cre, cim, d, wglu, bglu, *, ns, tb):
    nseq, t_len, c = u.shape
    n_state = h0r.shape[1]
    blk = pl.BlockSpec((ns, tb, c), lambda i, j: (i, j, 0))
    st = pl.BlockSpec((ns, n_state), lambda i, j: (i, 0))
    st_shape = jax.ShapeDtypeStruct((nseq, n_state), F32)
    consts = (pcr, pci, plr, pli, bre, bim, cre, cim, d, wglu, bglu)
    return pl.pallas_call(
        functools.partial(_s5_kernel, ns=ns, tb=tb),
        grid=(nseq // ns, t_len // tb),
        in_specs=[blk, st, st] + [_resident(a.shape) for a in consts],
        out_specs=[blk, st, st],
        out_shape=[jax.ShapeDtypeStruct((nseq, t_len, c), F32), st_shape, st_shape],
        scratch_shapes=[pltpu.VMEM((ns * tb, n_state), F32), pltpu.VMEM((ns * tb, n_state), F32)],
        compiler_params=_params("parallel", "arbitrary"),
        name="s5",
    )(u, h0r, h0i, *consts)


def _trunk(x, p, conv_a, s5_re, s5_im, conv_c, w, cfg):
    nseq, t_len, d = x.shape
    m = nseq * t_len
    depth = p.shape[0]
    flat = lambda a: a.reshape(m, a.shape[-1])
    seq = lambda a: a.reshape(nseq, t_len, a.shape[-1])
    x = flat(x)
    p = p.reshape(depth, m, p.shape[-1])
    new_ca, new_sr, new_si, new_cc = [], [], [], []
    for i in range(depth):
        j = i // 2
        h = _ffn(x, w["norm_ffn1"][i], w["w_ffn1_gate"], w["w_ffn1_up"], w["w_ffn1_down"], i)
        if i % 2 == 0:
            a, u = _in_ab(h, w["norm_mix"][i], w["w_in_ab"][j])
            a, ca = _conv_a(seq(a), conv_a[j], w["w_dw_a"][j], w["b_dw_a"][j], w["ln_a_g"][j], w["ln_a_b"][j],
                            **cfg["conv_a"])
            s5w = w["s5"][j]
            n_state = s5_re.shape[-2] * s5_re.shape[-1]
            yb, sr, si = _s5(seq(u), s5_re[j].reshape(nseq, n_state), s5_im[j].reshape(nseq, n_state), *s5w,
                             w["w_glu_b"][j], w["b_glu_b"][j], **cfg["s5"])
            h = _out_proj([flat(a), flat(yb)], w["w_out_ab"][j], h)
            new_ca.append(ca)
            new_sr.append(sr.reshape(s5_re.shape[1:]))
            new_si.append(si.reshape(s5_im.shape[1:]))
        else:
            bg, cv = _in_c(h, w["norm_mix"][i], w["w_in_c"][j])
            mix, cc = _conv_c(seq(bg), seq(cv), conv_c[j], w["w_dw_c"][j], **cfg["conv_c"])
            h = _out_proj([flat(mix)], w["w_out_c"][j], h)
            new_cc.append(cc)
        h = _ffn(h, w["norm_ffn2"][i], w["w_ffn2_gate"], w["w_ffn2_up"], w["w_ffn2_down"], i)
        x = _ple(h, p, i, w["norm_ple"][i], w["w_ple_gate"][i], w["w_ple_proj"][i],
                 w["norm_final"] if i == depth - 1 else None)
    return (x.reshape(nseq, t_len, d), jnp.stack(new_ca), jnp.stack(new_sr), jnp.stack(new_si),
            jnp.stack(new_cc))


def _prepare_weights(norm_ffn1, w_ffn1_gate, w_ffn1_up, w_ffn1_down, norm_mix, norm_ffn2, w_ffn2_gate, w_ffn2_up,
                     w_ffn2_down, norm_ple, w_ple_gate, w_ple_proj, w_in_ab, w_dw_a, b_dw_a, ln_a_g, ln_a_b,
                     s5_lam_re, s5_lam_im, s5_log_dt, s5_b_re, s5_b_im, s5_c_re, s5_c_im, s5_d, w_glu_b, b_glu_b,
                     w_out_ab, w_in_c, w_dw_c, w_out_c, norm_final):
    row = lambda a: a[..., None, :]
    bf = lambda a: a.astype(BF16)
    w = dict(
        norm_ffn1=row(norm_ffn1), norm_mix=row(norm_mix), norm_ffn2=row(norm_ffn2), norm_ple=row(norm_ple),
        norm_final=norm_final[None, :],
        w_ffn1_gate=w_ffn1_gate, w_ffn1_up=w_ffn1_up, w_ffn1_down=w_ffn1_down,
        w_ffn2_gate=w_ffn2_gate, w_ffn2_up=w_ffn2_up, w_ffn2_down=w_ffn2_down,
        w_ple_gate=bf(w_ple_gate), w_ple_proj=bf(w_ple_proj),
        w_in_ab=bf(w_in_ab), w_out_ab=bf(w_out_ab), w_glu_b=bf(w_glu_b), b_glu_b=row(b_glu_b),
        w_dw_a=w_dw_a, b_dw_a=row(b_dw_a), ln_a_g=row(ln_a_g), ln_a_b=row(ln_a_b),
        w_in_c=bf(w_in_c), w_dw_c=w_dw_c, w_out_c=bf(w_out_c),
    )
    s5 = []
    for j in range(s5_lam_re.shape[0]):
        pwr, pwi, bbr, bbi = _s5_prep(s5_lam_re[j], s5_lam_im[j], s5_log_dt[j], s5_b_re[j], s5_b_im[j])
        ct = lambda a: jnp.transpose(a, (0, 2, 1))
        pcr, plr = _scan_tiles(pwr)
        pci, pli = _scan_tiles(pwi)
        s5.append((pcr, pci, plr, pli,
                   bf(_block_diag_tiles(bbr)), bf(_block_diag_tiles(bbi)),
                   bf(_block_diag_tiles(ct(s5_c_re[j]))), bf(_block_diag_tiles(ct(s5_c_im[j]))),
                   s5_d[j][None, :]))
    w["s5"] = s5
    return w


def kernel(x_prompt, x_sample, p_prompt, p_sample, state_convA, state_s5_re, state_s5_im, state_convC, norm_ffn1, w_ffn1_gate, w_ffn1_up, w_ffn1_down, norm_mix, norm_ffn2, w_ffn2_gate, w_ffn2_up, w_ffn2_down, norm_ple, w_ple_gate, w_ple_proj, w_in_ab, w_dw_a, b_dw_a, ln_a_g, ln_a_b, s5_lam_re, s5_lam_im, s5_log_dt, s5_b_re, s5_b_im, s5_c_re, s5_c_im, s5_d, w_glu_b, b_glu_b, w_out_ab, w_in_c, w_dw_c, w_out_c, norm_final):
    w = _prepare_weights(norm_ffn1, w_ffn1_gate, w_ffn1_up, w_ffn1_down, norm_mix, norm_ffn2, w_ffn2_gate,
                         w_ffn2_up, w_ffn2_down, norm_ple, w_ple_gate, w_ple_proj, w_in_ab, w_dw_a, b_dw_a,
                         ln_a_g, ln_a_b, s5_lam_re, s5_lam_im, s5_log_dt, s5_b_re, s5_b_im, s5_c_re, s5_c_im,
                         s5_d, w_glu_b, b_glu_b, w_out_ab, w_in_c, w_dw_c, w_out_c, norm_final)
    bp = x_prompt.shape[0]
    n_even, n_odd = state_convA.shape[0], state_convC.shape[0]
    ca0 = jnp.zeros((n_even, bp) + state_convA.shape[2:], F32)
    sr0 = jnp.zeros((n_even, bp) + state_s5_re.shape[2:], F32)
    si0 = jnp.zeros((n_even, bp) + state_s5_im.shape[2:], F32)
    cc0 = jnp.zeros((n_odd, bp) + state_convC.shape[2:], F32)
    cfg_prompt = dict(conv_a=dict(ns=1, tb=256, rows=32, shift=True), conv_c=dict(ns=1, tb=256, rows=16),
                      s5=dict(ns=bp, tb=128))
    t_s = x_sample.shape[1]
    cfg_sample = dict(conv_a=dict(ns=16, tb=t_s, rows=t_s, shift=False), conv_c=dict(ns=16, tb=t_s, rows=t_s),
                      s5=dict(ns=32, tb=t_s))
    y_p, ca_p, sr_p, si_p, cc_p = _trunk(x_prompt, p_prompt, ca0, sr0, si0, cc0, w, cfg_prompt)
    y_s, ca_s, sr_s, si_s, cc_s = _trunk(x_sample, p_sample, state_convA, state_s5_re, state_s5_im, state_convC, w,
                                         cfg_sample)
    return (y_p, y_s, ca_p, ca_s, sr_p, si_p, sr_s, si_s, cc_p, cc_s)
```

```python
import functools

import jax
import jax.numpy as jnp
from jax import lax
from jax.experimental import pallas as pl
from jax.experimental.pallas import tpu as pltpu

F32 = jnp.float32
BF16 = jnp.bfloat16
EPS = 1e-6

LANES = 128
SUBLANES = 8
VMEM_LIMIT = 56 * 1024 * 1024

S5_GROUP = 16
S5_STATE = 64
S5_TILE_GROUPS = 16
SCAN_SHIFTS = (1, 2, 4)
S5_SCAN_SEQS = 4
S5_SCAN_LANES = 4 * LANES
CONV_A_WIDTH = 31
CONV_C_WIDTH = 3
CONV_A_PAD = 32
CONV_C_PAD = 8
FFN_DOWN_COLS = 512
CONV_SHIFT_ROWS = 40


def _dot(a, b):
    return jnp.dot(a, b, preferred_element_type=F32)


def _rmsnorm_bf16(x, g):
    ms = jnp.mean(x * x, axis=-1, keepdims=True)
    return (x * lax.rsqrt(ms + EPS) * g).astype(BF16)


def _params(*sem):
    return pltpu.CompilerParams(dimension_semantics=sem, vmem_limit_bytes=VMEM_LIMIT)


def _resident(shape):
    nd = len(shape)
    return pl.BlockSpec(shape, lambda *_: (0,) * nd, pipeline_mode=pl.Buffered(1))


def _ffn_kernel(x_ref, g_ref, wg_ref, wu_ref, wd_ref, o_ref, xn_ref):
    @pl.when(pl.program_id(1) == 0)
    def _():
        x = x_ref[...]
        xn_ref[...] = _rmsnorm_bf16(x, g_ref[...])
        o_ref[...] = x

    xn = xn_ref[...]
    gate = _dot(xn, wg_ref[...].astype(BF16))
    up = _dot(xn, wu_ref[...].astype(BF16))
    act = (gate * jax.nn.sigmoid(gate) * (0.5 * up)).astype(BF16)
    wd = wd_ref[...].astype(BF16)
    for c in range(o_ref.shape[1] // FFN_DOWN_COLS):
        cols = slice(c * FFN_DOWN_COLS, (c + 1) * FFN_DOWN_COLS)
        o_ref[:, cols] += _dot(act, wd[:, cols])


def _ffn(x, g, wg, wu, wd, layer, *, tm=1024, tf=256):
    m, d = x.shape
    f = wg.shape[-1]
    return pl.pallas_call(
        _ffn_kernel,
        grid=(m // tm, f // tf),
        in_specs=[
            pl.BlockSpec((tm, d), lambda i, j: (i, 0)),
            pl.BlockSpec((1, d), lambda i, j: (0, 0)),
            pl.BlockSpec((None, d, tf), lambda i, j: (layer, 0, j)),
            pl.BlockSpec((None, d, tf), lambda i, j: (layer, 0, j)),
            pl.BlockSpec((None, tf, d), lambda i, j: (layer, j, 0)),
        ],
        out_specs=pl.BlockSpec((tm, d), lambda i, j: (i, 0)),
        out_shape=jax.ShapeDtypeStruct((m, d), F32),
        scratch_shapes=[pltpu.VMEM((tm, d), BF16)],
        compiler_params=_params("parallel", "arbitrary"),
        name="ffn",
    )(x, g, wg, wu, wd)


def _inab_kernel(x_ref, g_ref, wv_ref, wgt_ref, wu_ref, a_ref, u_ref, xn_ref):
    @pl.when(pl.program_id(1) == 0)
    def _():
        xn_ref[...] = _rmsnorm_bf16(x_ref[...], g_ref[...])

    xn = xn_ref[...]
    bf = lambda w_ref: w_ref[...].astype(BF16)
    a_ref[...] = _dot(xn, bf(wv_ref)) * jax.nn.sigmoid(_dot(xn, bf(wgt_ref)))
    u_ref[...] = _dot(xn, bf(wu_ref))


def _in_ab(x, g, w, *, tm=1024, tn=256):
    m, d = x.shape
    c = w.shape[1] // 3
    nb = c // tn
    wspec = lambda k: pl.BlockSpec((d, tn), lambda i, j: (0, j + k * nb))
    ospec = pl.BlockSpec((tm, tn), lambda i, j: (i, j))
    return pl.pallas_call(
        _inab_kernel,
        grid=(m // tm, nb),
        in_specs=[pl.BlockSpec((tm, d), lambda i, j: (i, 0)), pl.BlockSpec((1, d), lambda i, j: (0, 0)),
                  wspec(0), wspec(1), wspec(2)],
        out_specs=[ospec, ospec],
        out_shape=[jax.ShapeDtypeStruct((m, c), F32)] * 2,
        scratch_shapes=[pltpu.VMEM((tm, d), BF16)],
        compiler_params=_params("parallel", "arbitrary"),
        name="in_ab",
    )(x, g, w, w, w)


def _inc_kernel(x_ref, g_ref, wb_ref, wc_ref, wv_ref, bg_ref, cv_ref, xn_ref):
    @pl.when(pl.program_id(1) == 0)
    def _():
        xn_ref[...] = _rmsnorm_bf16(x_ref[...], g_ref[...])

    xn = xn_ref[...]
    bf = lambda w_ref: w_ref[...].astype(BF16)
    bg_ref[...] = _dot(xn, bf(wb_ref))
    cv_ref[...] = _dot(xn, bf(wc_ref)) * _dot(xn, bf(wv_ref))


def _in_c(x, g, w, *, tm=1024, tn=256):
    m, d = x.shape
    c = w.shape[1] // 3
    nb = c // tn
    wspec = lambda k: pl.BlockSpec((d, tn), lambda i, j: (0, j + k * nb))
    ospec = pl.BlockSpec((tm, tn), lambda i, j: (i, j))
    return pl.pallas_call(
        _inc_kernel,
        grid=(m // tm, nb),
        in_specs=[pl.BlockSpec((tm, d), lambda i, j: (i, 0)), pl.BlockSpec((1, d), lambda i, j: (0, 0)),
                  wspec(0), wspec(1), wspec(2)],
        out_specs=[ospec, ospec],
        out_shape=[jax.ShapeDtypeStruct((m, c), F32)] * 2,
        scratch_shapes=[pltpu.VMEM((tm, d), BF16)],
        compiler_params=_params("parallel", "arbitrary"),
        name="in_c",
    )(x, g, w, w, w)


def _outproj_kernel(*refs, ksizes, tn):
    n = len(ksizes)
    act_refs, (w_ref, h_ref, o_ref) = refs[:n], refs[n:]
    acts = [a[...].astype(BF16) for a in act_refs]
    for c in range(o_ref.shape[1] // tn):
        cols = slice(c * tn, (c + 1) * tn)
        acc = h_ref[:, cols]
        off = 0
        for a, k in zip(acts, ksizes):
            acc = acc + _dot(a, w_ref[off:off + k, cols])
            off += k
        o_ref[:, cols] = acc


def _out_proj(acts, w, h, *, tm=512, tn=512):
    m, d = h.shape
    ksizes = tuple(a.shape[1] for a in acts)
    row = lambda k: pl.BlockSpec((tm, k), lambda i: (i, 0))
    return pl.pallas_call(
        functools.partial(_outproj_kernel, ksizes=ksizes, tn=tn),
        grid=(m // tm,),
        in_specs=[row(k) for k in ksizes] + [_resident(w.shape), row(d)],
        out_specs=row(d),
        out_shape=jax.ShapeDtypeStruct((m, d), F32),
        compiler_params=_params("parallel"),
        name="out_proj",
    )(*acts, w, h)


def _ple_kernel(*refs, final, tn):
    if final:
        x_ref, p_ref, g_ref, wg_ref, wp_ref, gf_ref, o_ref = refs
    else:
        x_ref, p_ref, g_ref, wg_ref, wp_ref, o_ref = refs
    d = o_ref.shape[1]
    xn = _rmsnorm_bf16(x_ref[...], g_ref[...])
    pb = p_ref[...].astype(BF16)
    ssq = jnp.zeros((o_ref.shape[0], 1), F32)
    for c in range(d // tn):
        cols = slice(c * tn, (c + 1) * tn)
        gate = jax.nn.sigmoid(_dot(xn, wg_ref[:, cols]))
        y = x_ref[:, cols] + gate * _dot(pb, wp_ref[:, cols])
        ssq = ssq + jnp.sum(y * y, axis=-1, keepdims=True)
        o_ref[:, cols] = y
    if final:
        o_ref[...] = o_ref[...] * lax.rsqrt(ssq / d + EPS) * gf_ref[...]


def _ple(x, p, layer, g, wg, wp, gf, *, tm=512, tn=512):
    m, d = x.shape
    dp = p.shape[-1]
    final = gf is not None
    in_specs = [
        pl.BlockSpec((tm, d), lambda i: (i, 0)),
        pl.BlockSpec((None, tm, dp), lambda i: (layer, i, 0)),
        _resident((1, d)),
        _resident(wg.shape),
        _resident(wp.shape),
    ]
    args = [x, p, g, wg, wp]
    if final:
        in_specs.append(_resident((1, d)))
        args.append(gf)
    return pl.pallas_call(
        functools.partial(_ple_kernel, final=final, tn=tn),
        grid=(m // tm,),
        in_specs=in_specs,
        out_specs=pl.BlockSpec((tm, d), lambda i: (i, 0)),
        out_shape=jax.ShapeDtypeStruct((m, d), F32),
        compiler_params=_params("parallel"),
        name="ple",
    )(*args)


def _conva_kernel(x_ref, prev_ref, w_ref, b_ref, g_ref, beta_ref, o_ref, st_ref, buf_ref, *shift_refs, ns, tb, nt, rows):
    t = pl.program_id(1)
    pad, k_w = CONV_A_PAD, CONV_A_WIDTH
    first = pad - (k_w - 1)

    @pl.when(t == 0)
    def _():
        buf_ref[:, 0:pad, :] = prev_ref[...]

    buf_ref[:, pad:pad + tb, :] = x_ref[...]

    def tap(s, r, k):
        off = first + k
        if not shift_refs:
            return buf_ref[s, pl.ds(r * rows + off, rows), :]
        j = off % SUBLANES
        if j == 0:
            return buf_ref[s, pl.ds(r * rows + off, rows), :]
        return shift_refs[0][j - 1, pl.ds(r * rows + off - j, rows), :]

    def per_seq(s, carry):
        if shift_refs:
            span = shift_refs[0].shape[1]
            for j in range(1, SUBLANES):
                for r0 in range(0, span, CONV_SHIFT_ROWS):
                    shift_refs[0][j - 1, r0:r0 + CONV_SHIFT_ROWS, :] = buf_ref[s, pl.ds(r0 + j, CONV_SHIFT_ROWS), :]
        for r in range(tb // rows):
            acc = None
            for k in range(k_w):
                term = tap(s, r, k) * w_ref[k:k + 1, :]
                acc = term if acc is None else acc + term
            y = acc + b_ref[...]
            mu = jnp.mean(y, axis=-1, keepdims=True)
            yc = y - mu
            var = jnp.mean(yc * yc, axis=-1, keepdims=True)
            z = yc * lax.rsqrt(var + EPS) * g_ref[...] + beta_ref[...]
            o_ref[s, r * rows:(r + 1) * rows, :] = z * jax.nn.sigmoid(z)
        return carry

    lax.fori_loop(0, ns, per_seq, 0)

    @pl.when(t == nt - 1)
    def _():
        st_ref[...] = buf_ref[:, tb + first:tb + pad, :]

    if nt > 1:
        buf_ref[:, 0:pad, :] = buf_ref[:, tb:tb + pad, :]


def _conv_a(x, prev, w, b, g, beta, *, ns, tb, rows, shift):
    nseq, t_len, c = x.shape
    nt = t_len // tb
    prev = jnp.pad(prev, ((0, 0), (CONV_A_PAD - (CONV_A_WIDTH - 1), 0), (0, 0)))
    scratch = [pltpu.VMEM((ns, CONV_A_PAD + tb, c), F32)]
    if shift:
        span = tb + CONV_A_PAD - SUBLANES
        assert span % CONV_SHIFT_ROWS == 0
        scratch.append(pltpu.VMEM((SUBLANES - 1, span, c), F32))
    return pl.pallas_call(
        functools.partial(_conva_kernel, ns=ns, tb=tb, nt=nt, rows=rows),
        grid=(nseq // ns, nt),
        in_specs=[
            pl.BlockSpec((ns, tb, c), lambda i, j: (i, j, 0)),
            pl.BlockSpec((ns, CONV_A_PAD, c), lambda i, j: (i, 0, 0)),
            pl.BlockSpec((CONV_A_WIDTH, c), lambda i, j: (0, 0)),
            pl.BlockSpec((1, c), lambda i, j: (0, 0)),
            pl.BlockSpec((1, c), lambda i, j: (0, 0)),
            pl.BlockSpec((1, c), lambda i, j: (0, 0)),
        ],
        out_specs=[
            pl.BlockSpec((ns, tb, c), lambda i, j: (i, j, 0)),
            pl.BlockSpec((ns, CONV_A_WIDTH - 1, c), lambda i, j: (i, 0, 0)),
        ],
        out_shape=[jax.ShapeDtypeStruct((nseq, t_len, c), F32),
                   jax.ShapeDtypeStruct((nseq, CONV_A_WIDTH - 1, c), F32)],
        scratch_shapes=scratch,
        compiler_params=_params("parallel", "arbitrary"),
        name="conv_a",
    )(x, prev, w, b, g, beta)


def _convc_kernel(bg_ref, cv_ref, prev_ref, w_ref, o_ref, st_ref, buf_ref, *, ns, tb, nt, rows):
    t = pl.program_id(1)
    pad, k_w = CONV_C_PAD, CONV_C_WIDTH
    first = pad - (k_w - 1)

    @pl.when(t == 0)
    def _():
        buf_ref[:, 0:pad, :] = prev_ref[...]

    buf_ref[:, pad:pad + tb, :] = cv_ref[...]

    def per_seq(s, carry):
        for r in range(tb // rows):
            acc = None
            for k in range(k_w):
                term = buf_ref[s, pl.ds(r * rows + first + k, rows), :] * w_ref[k:k + 1, :]
                acc = term if acc is None else acc + term
            o_ref[s, r * rows:(r + 1) * rows, :] = bg_ref[s, r * rows:(r + 1) * rows, :] * acc
        return carry

    lax.fori_loop(0, ns, per_seq, 0)

    @pl.when(t == nt - 1)
    def _():
        st_ref[...] = buf_ref[:, tb + first:tb + pad, :]

    if nt > 1:
        buf_ref[:, 0:pad, :] = buf_ref[:, tb:tb + pad, :]


def _conv_c(bg, cv, prev, w, *, ns, tb, rows):
    nseq, t_len, c = cv.shape
    nt = t_len // tb
    prev = jnp.pad(prev, ((0, 0), (CONV_C_PAD - (CONV_C_WIDTH - 1), 0), (0, 0)))
    blk = pl.BlockSpec((ns, tb, c), lambda i, j: (i, j, 0))
    return pl.pallas_call(
        functools.partial(_convc_kernel, ns=ns, tb=tb, nt=nt, rows=rows),
        grid=(nseq // ns, nt),
        in_specs=[blk, blk,
                  pl.BlockSpec((ns, CONV_C_PAD, c), lambda i, j: (i, 0, 0)),
                  pl.BlockSpec((CONV_C_WIDTH, c), lambda i, j: (0, 0))],
        out_specs=[blk, pl.BlockSpec((ns, CONV_C_WIDTH - 1, c), lambda i, j: (i, 0, 0))],
        out_shape=[jax.ShapeDtypeStruct((nseq, t_len, c), F32),
                   jax.ShapeDtypeStruct((nseq, CONV_C_WIDTH - 1, c), F32)],
        scratch_shapes=[pltpu.VMEM((ns, CONV_C_PAD + tb, c), F32)],
        compiler_params=_params("parallel", "arbitrary"),
        name="conv_c",
    )(bg, cv, prev, w)


def _s5prep_kernel(lr_ref, li_ref, ldt_ref, br_ref, bi_ref, pwr_ref, pwi_ref, bbr_ref, bbi_ref):
    lr, li = lr_ref[...], li_ref[...]
    dt = jnp.exp(ldt_ref[...])
    mag = jnp.exp(lr * dt)
    ar, ai = mag * jnp.cos(li * dt), mag * jnp.sin(li * dt)
    den = lr * lr + li * li
    qr = ((ar - 1.0) * lr + ai * li) / den
    qi = (ai * lr - (ar - 1.0) * li) / den
    br, bi = br_ref[...], bi_ref[...]
    bbr_ref[...] = qr * br - qi * bi
    bbi_ref[...] = qr * bi + qi * br
    pr, pi = ar, ai
    for r in range(SUBLANES):
        pwr_ref[:, r:r + 1, :] = pr
        pwi_ref[:, r:r + 1, :] = pi
        pr, pi = pr * ar - pi * ai, pr * ai + pi * ar


def _s5_prep(lam_re, lam_im, log_dt, b_re, b_im):
    g, n = lam_re.shape
    h = b_re.shape[-1]
    lam3 = lambda a: a.reshape(g, 1, n)
    ldt = jnp.broadcast_to(log_dt.reshape(g, 1, 1), (g, 1, n))
    bt = lambda a: jnp.transpose(a, (0, 2, 1))
    small = jax.ShapeDtypeStruct((g, SUBLANES, n), F32)
    big = jax.ShapeDtypeStruct((g, h, n), F32)
    return pl.pallas_call(_s5prep_kernel, out_shape=[small, small, big, big], name="s5_prep")(
        lam3(lam_re), lam3(lam_im), ldt, bt(b_re), bt(b_im))


def _scan_tiles(pw):
    g, r, n = pw.shape
    carry = jnp.transpose(pw, (1, 0, 2)).reshape(r, g * n)
    row = jnp.arange(r)[:, None]
    levels = [jnp.where(row >= sh, carry[sh - 1:sh, :], 0.0) for sh in SCAN_SHIFTS]
    return carry, jnp.stack(levels)


def _block_diag_tiles(x):
    g, r, c = x.shape
    tg = S5_TILE_GROUPS
    eye = jnp.eye(tg, dtype=x.dtype)
    y = x.reshape(g // tg, tg, r, 1, c) * eye[None, :, None, :, None]
    return y.reshape(g // tg, tg * r, tg * c)


def _s5_kernel(u_ref, h0r_ref, h0i_ref, pcr_ref, pci_ref, plr_ref, pli_ref, bre_ref, bim_ref, cre_ref, cim_ref, d_ref,
               wglu_ref, bglu_ref, y_ref, hr_ref, hi_ref, bur_ref, bui_ref, *, ns, tb):
    n_tiles, k_in, k_st = bre_ref.shape
    rows = ns * tb
    n_state = n_tiles * k_st
    width = S5_SCAN_LANES
    t_tiles = tb // SUBLANES

    @pl.when(pl.program_id(1) == 0)
    def _():
        hr_ref[...] = h0r_ref[...]
        hi_ref[...] = h0i_ref[...]

    u = u_ref[...].reshape(rows, n_tiles * k_in)
    ub = u.astype(BF16)
    for k in range(n_tiles):
        uk = ub[:, k * k_in:(k + 1) * k_in]
        bur_ref[:, k * k_st:(k + 1) * k_st] = _dot(uk, bre_ref[k])
        bui_ref[:, k * k_st:(k + 1) * k_st] = _dot(uk, bim_ref[k])

    def scan_group(grp):
        for c in range(n_state // width):
            lanes = slice(c * width, (c + 1) * width)

            def tile(i, carry, lanes=lanes):
                out = []
                for s in range(S5_SCAN_SEQS):
                    hr, hi = carry[2 * s], carry[2 * s + 1]
                    at = pl.ds(pl.multiple_of((grp * S5_SCAN_SEQS + s) * tb + i * SUBLANES, SUBLANES), SUBLANES)
                    xr, xi = bur_ref[at, lanes], bui_ref[at, lanes]
                    for k, sh in enumerate(SCAN_SHIFTS):
                        sr, si = pltpu.roll(xr, sh, axis=0), pltpu.roll(xi, sh, axis=0)
                        ar, ai = plr_ref[k, :, lanes], pli_ref[k, :, lanes]
                        xr, xi = xr + (ar * sr - ai * si), xi + (ar * si + ai * sr)
                    cr, ci = jnp.broadcast_to(hr, xr.shape), jnp.broadcast_to(hi, xi.shape)
                    pr, pi = pcr_ref[:, lanes], pci_ref[:, lanes]
                    xr, xi = xr + (pr * cr - pi * ci), xi + (pr * ci + pi * cr)
                    bur_ref[at, lanes] = xr
                    bui_ref[at, lanes] = xi
                    out += [xr[SUBLANES - 1:, :], xi[SUBLANES - 1:, :]]
                return tuple(out)

            init = []
            for s in range(S5_SCAN_SEQS):
                seq = pl.ds(grp * S5_SCAN_SEQS + s, 1)
                init += [hr_ref[seq, lanes], hi_ref[seq, lanes]]
            last = tile(0, tuple(init)) if t_tiles == 1 else lax.fori_loop(0, t_tiles, tile, tuple(init))
            for s in range(S5_SCAN_SEQS):
                seq = pl.ds(grp * S5_SCAN_SEQS + s, 1)
                hr_ref[seq, lanes] = last[2 * s]
                hi_ref[seq, lanes] = last[2 * s + 1]

    n_groups = ns // S5_SCAN_SEQS
    if n_groups == 1:
        scan_group(0)
    else:
        def group_step(grp, carry):
            scan_group(grp)
            return carry
        lax.fori_loop(0, n_groups, group_step, 0)

    ys = []
    for k in range(n_tiles):
        st = slice(k * k_st, (k + 1) * k_st)
        ys.append(_dot(bur_ref[:, st].astype(BF16), cre_ref[k]) - _dot(bui_ref[:, st].astype(BF16), cim_ref[k]))
    y = jnp.concatenate(ys, axis=-1) + d_ref[...] * u
    y = jax.nn.gelu(y)
    y = y * jax.nn.sigmoid(_dot(y.astype(BF16), wglu_ref[...]) + bglu_ref[...])
    y_ref[...] = y.reshape(y_ref.shape)


def _s5(u, h0r, h0i, pcr, pci, plr, pli, bre, bim, cre, cim, d, wglu, bglu, *, ns, tb):
    nseq, t_len, c = u.shape
    n_state = h0r.shape[1]
    blk = pl.BlockSpec((ns, tb, c), lambda i, j: (i, j, 0))
    st = pl.BlockSpec((ns, n_state), lambda i, j: (i, 0))
    st_shape = jax.ShapeDtypeStruct((nseq, n_state), F32)
    consts = (pcr, pci, plr, pli, bre, bim, cre, cim, d, wglu, bglu)
    return pl.pallas_call(
        functools.partial(_s5_kernel, ns=ns, tb=tb),
        grid=(nseq // ns, t_len // tb),
        in_specs=[blk, st, st] + [_resident(a.shape) for a in consts],
        out_specs=[blk, st, st],
        out_shape=[jax.ShapeDtypeStruct((nseq, t_len, c), F32), st_shape, st_shape],
        scratch_shapes=[pltpu.VMEM((ns * tb, n_state), F32), pltpu.VMEM((ns * tb, n_state), F32)],
        compiler_params=_params("parallel", "arbitrary"),
        name="s5",
    )(u, h0r, h0i, *consts)


def _trunk(x, p, conv_a, s5_re, s5_im, conv_c, w, cfg):
    nseq, t_len, d = x.shape
    m = nseq * t_len
    depth = p.shape[0]
    flat = lambda a: a.reshape(m, a.shape[-1])
    seq = lambda a: a.reshape(nseq, t_len, a.shape[-1])
    x = flat(x)
    p = p.reshape(depth, m, p.shape[-1])
    new_ca, new_sr, new_si, new_cc = [], [], [], []
    for i in range(depth):
        j = i // 2
        h = _ffn(x, w["norm_ffn1"][i], w["w_ffn1_gate"], w["w_ffn1_up"], w["w_ffn1_down"], i)
        if i % 2 == 0:
            a, u = _in_ab(h, w["norm_mix"][i], w["w_in_ab"][j])
            a, ca = _conv_a(seq(a), conv_a[j], w["w_dw_a"][j], w["b_dw_a"][j], w["ln_a_g"][j], w["ln_a_b"][j],
                            **cfg["conv_a"])
            s5w = w["s5"][j]
            n_state = s5_re.shape[-2] * s5_re.shape[-1]
            yb, sr, si = _s5(seq(u), s5_re[j].reshape(nseq, n_state), s5_im[j].reshape(nseq, n_state), *s5w,
                             w["w_glu_b"][j], w["b_glu_b"][j], **cfg["s5"])
            h = _out_proj([flat(a), flat(yb)], w["w_out_ab"][j], h)
            new_ca.append(ca)
            new_sr.append(sr.reshape(s5_re.shape[1:]))
            new_si.append(si.reshape(s5_im.shape[1:]))
        else:
            bg, cv = _in_c(h, w["norm_mix"][i], w["w_in_c"][j])
            mix, cc = _conv_c(seq(bg), seq(cv), conv_c[j], w["w_dw_c"][j], **cfg["conv_c"])
            h = _out_proj([flat(mix)], w["w_out_c"][j], h)
            new_cc.append(cc)
        h = _ffn(h, w["norm_ffn2"][i], w["w_ffn2_gate"], w["w_ffn2_up"], w["w_ffn2_down"], i)
        x = _ple(h, p, i, w["norm_ple"][i], w["w_ple_gate"][i], w["w_ple_proj"][i],
                 w["norm_final"] if i == depth - 1 else None)
    return (x.reshape(nseq, t_len, d), jnp.stack(new_ca), jnp.stack(new_sr), jnp.stack(new_si),
            jnp.stack(new_cc))


def _prepare_weights(norm_ffn1, w_ffn1_gate, w_ffn1_up, w_ffn1_down, norm_mix, norm_ffn2, w_ffn2_gate, w_ffn2_up,
                     w_ffn2_down, norm_ple, w_ple_gate, w_ple_proj, w_in_ab, w_dw_a, b_dw_a, ln_a_g, ln_a_b,
                     s5_lam_re, s5_lam_im, s5_log_dt, s5_b_re, s5_b_im, s5_c_re, s5_c_im, s5_d, w_glu_b, b_glu_b,
                     w_out_ab, w_in_c, w_dw_c, w_out_c, norm_final):
    row = lambda a: a[..., None, :]
    bf = lambda a: a.astype(BF16)
    w = dict(
        norm_ffn1=row(norm_ffn1), norm_mix=row(norm_mix), norm_ffn2=row(norm_ffn2), norm_ple=row(norm_ple),
        norm_final=norm_final[None, :],
        w_ffn1_gate=w_ffn1_gate, w_ffn1_up=w_ffn1_up, w_ffn1_down=w_ffn1_down,
        w_ffn2_gate=w_ffn2_gate, w_ffn2_up=w_ffn2_up, w_ffn2_down=w_ffn2_down,
        w_ple_gate=bf(w_ple_gate), w_ple_proj=bf(w_ple_proj),
        w_in_ab=w_in_ab, w_out_ab=bf(w_out_ab), w_glu_b=bf(w_glu_b), b_glu_b=row(b_glu_b),
        w_dw_a=w_dw_a, b_dw_a=row(b_dw_a), ln_a_g=row(ln_a_g), ln_a_b=row(ln_a_b),
        w_in_c=w_in_c, w_dw_c=w_dw_c, w_out_c=bf(w_out_c),
    )
    s5 = []
    for j in range(s5_lam_re.shape[0]):
        pwr, pwi, bbr, bbi = _s5_prep(s5_lam_re[j], s5_lam_im[j], s5_log_dt[j], s5_b_re[j], s5_b_im[j])
        ct = lambda a: jnp.transpose(a, (0, 2, 1))
        pcr, plr = _scan_tiles(pwr)
        pci, pli = _scan_tiles(pwi)
        s5.append((pcr, pci, plr, pli,
                   bf(_block_diag_tiles(bbr)), bf(_block_diag_tiles(bbi)),
                   bf(_block_diag_tiles(ct(s5_c_re[j]))), bf(_block_diag_tiles(ct(s5_c_im[j]))),
                   s5_d[j][None, :]))
    w["s5"] = s5
    return w


def kernel(x_prompt, x_sample, p_prompt, p_sample, state_convA, state_s5_re, state_s5_im, state_convC, norm_ffn1, w_ffn1_gate, w_ffn1_up, w_ffn1_down, norm_mix, norm_ffn2, w_ffn2_gate, w_ffn2_up, w_ffn2_down, norm_ple, w_ple_gate, w_ple_proj, w_in_ab, w_dw_a, b_dw_a, ln_a_g, ln_a_b, s5_lam_re, s5_lam_im, s5_log_dt, s5_b_re, s5_b_im, s5_c_re, s5_c_im, s5_d, w_glu_b, b_glu_b, w_out_ab, w_in_c, w_dw_c, w_out_c, norm_final):
    w = _prepare_weights(norm_ffn1, w_ffn1_gate, w_ffn1_up, w_ffn1_down, norm_mix, norm_ffn2, w_ffn2_gate,
                         w_ffn2_up, w_ffn2_down, norm_ple, w_ple_gate, w_ple_proj, w_in_ab, w_dw_a, b_dw_a,
                         ln_a_g, ln_a_b, s5_lam_re, s5_lam_im, s5_log_dt, s5_b_re, s5_b_im, s5_c_re, s5_c_im,
                         s5_d, w_glu_b, b_glu_b, w_out_ab, w_in_c, w_dw_c, w_out_c, norm_final)
    bp = x_prompt.shape[0]
    n_even, n_odd = state_convA.shape[0], state_convC.shape[0]
    ca0 = jnp.zeros((n_even, bp) + state_convA.shape[2:], F32)
    sr0 = jnp.zeros((n_even, bp) + state_s5_re.shape[2:], F32)
    si0 = jnp.zeros((n_even, bp) + state_s5_im.shape[2:], F32)
    cc0 = jnp.zeros((n_odd, bp) + state_convC.shape[2:], F32)
    cfg_prompt = dict(conv_a=dict(ns=1, tb=256, rows=32, shift=True), conv_c=dict(ns=1, tb=256, rows=16),
                      s5=dict(ns=bp, tb=128))
    t_s = x_sample.shape[1]
    cfg_sample = dict(conv_a=dict(ns=16, tb=t_s, rows=t_s, shift=False), conv_c=dict(ns=16, tb=t_s, rows=t_s),
                      s5=dict(ns=32, tb=t_s))
    y_p, ca_p, sr_p, si_p, cc_p = _trunk(x_prompt, p_prompt, ca0, sr0, si0, cc0, w, cfg_prompt)
    y_s, ca_s, sr_s, si_s, cc_s = _trunk(x_sample, p_sample, state_convA, state_s5_re, state_s5_im, state_convC, w,
                                         cfg_sample)
    return (y_p, y_s, ca_p, ca_s, sr_p, si_p, sr_s, si_s, cc_p, cc_s)
```
